```python
import jax, jax.numpy as jnp
from jax import lax
import numpy as np

D_MODEL = 2048
BATCH = 8
SEQ = 2048
DEPTH = 2

HEAD_DIM = 128
DIL_CONFIGS = ((128, 1), (512, 4), (2048, 16))
N_DIL_GROUPS = 3
HEADS_PER_DIL_GROUP = 4
N_SB_HEADS = 8
SB_BLOCK = 128
ROPE_THETA = 500000.0
ROPE_DIM = HEAD_DIM // 4
N_EXPERT_GROUPS = 4
EXPERTS_PER_GROUP = 4
N_EXPERTS = N_EXPERT_GROUPS * EXPERTS_PER_GROUP
TOP_K_FINE = 2
EXPERT_FF = D_MODEL // 4
LN_EPS = 1e-5
DEEPNORM_ALPHA = (2 * DEPTH) ** 0.25
DEEPNORM_BETA = (8 * DEPTH) ** -0.25

WIDTH_A = N_DIL_GROUPS * HEADS_PER_DIL_GROUP * HEAD_DIM
OUT_A = HEADS_PER_DIL_GROUP * HEAD_DIM
WIDTH_B = N_SB_HEADS * HEAD_DIM
IN_SPLITS = (WIDTH_A, WIDTH_A, WIDTH_A, WIDTH_B, WIDTH_B, WIDTH_B, D_MODEL, D_MODEL)
IN_WIDTH = 3 * WIDTH_A + 3 * WIDTH_B + 2 * D_MODEL

kernel_name = "dilated_stickbreak_hier_moe_deepnorm"


def layer_norm(x, g, b):
    xf = x.astype(jnp.float32)
    mu = jnp.mean(xf, axis=-1, keepdims=True)
    var = jnp.mean(jnp.square(xf - mu), axis=-1, keepdims=True)
    return ((xf - mu) * lax.rsqrt(var + LN_EPS) * g + b).astype(x.dtype)


def partial_rotary(t, positions):
    half = ROPE_DIM // 2
    inv_freq = ROPE_THETA ** (-2.0 * jnp.arange(half, dtype=jnp.float32) / ROPE_DIM)
    ang = positions.astype(jnp.float32)[:, :, None] * inv_freq
    cos = jnp.cos(ang)[:, :, None, None, :]
    sin = jnp.sin(ang)[:, :, None, None, :]
    tf = t.astype(jnp.float32)
    x1, x2 = tf[..., :half], tf[..., half:ROPE_DIM]
    rot = jnp.concatenate([x1 * cos - x2 * sin, x2 * cos + x1 * sin, tf[..., ROPE_DIM:]], axis=-1)
    return rot.astype(t.dtype)


def dilated_window_attention(q, k, v, window, dilation):
    B, S, H, Dh = q.shape
    n = window // dilation
    sub_len = -(-S // dilation)
    sub_len = -(-sub_len // n) * n
    s_pad = sub_len * dilation
    nb = sub_len // n

    def to_blocks(t):
        t = jnp.pad(t, ((0, 0), (0, s_pad - S), (0, 0), (0, 0)))
        t = t.reshape(B, sub_len, dilation, H, Dh).transpose(0, 2, 3, 1, 4)
        return t.reshape(B, dilation, H, nb, n, Dh)

    def with_prev(t):
        prev = jnp.concatenate([jnp.zeros_like(t[:, :, :, :1]), t[:, :, :, :-1]], axis=3)
        return jnp.concatenate([prev, t], axis=4)

    qb = to_blocks(q)
    kk = with_prev(to_blocks(k))
    vv = with_prev(to_blocks(v))
    s = jnp.einsum('brhnqd,brhnkd->brhnqk', qb, kk).astype(jnp.float32) * (Dh ** -0.5)
    qi = jnp.arange(n)[:, None]
    kj = jnp.arange(2 * n)[None, :]
    band = (kj >= qi) & (kj <= qi + n)
    in_range = (jnp.arange(nb)[:, None, None] > 0) | (kj >= n)[None]
    valid = band[None] & in_range
    s = jnp.where(valid, s, -jnp.inf)
    m = jnp.max(s, axis=-1, keepdims=True)
    p = jnp.exp(s - m)
    den = jnp.sum(p, axis=-1, keepdims=True)
    out = jnp.einsum('brhnqk,brhnkd->brhnqd', (p / den).astype(v.dtype), vv)
    lse = (m + jnp.log(den))[..., 0]
    out = out.reshape(B, dilation, H, sub_len, Dh).transpose(0, 3, 1, 2, 4).reshape(B, s_pad, H, Dh)[:, :S]
    lse = lse.reshape(B, dilation, H, sub_len).transpose(0, 3, 1, 2).reshape(B, s_pad, H)[:, :S]
    return out, lse


def stick_breaking_attention(q, k, v):
    B, S, H, Dh = q.shape
    outs = []
    for i in range(S // SB_BLOCK):
        q0 = i * SB_BLOCK
        kv_len = q0 + SB_BLOCK
        z = jnp.einsum('bqhd,bkhd->bhqk', q[:, q0:kv_len], k[:, :kv_len]).astype(jnp.float32) * (Dh ** -0.5)
        t_idx = q0 + jnp.arange(SB_BLOCK)
        causal = jnp.arange(kv_len)[None, :] < t_idx[:, None]
        log_keep = jnp.where(causal, jax.nn.log_sigmoid(-z), 0.0)
        later = lax.cumsum(log_keep, axis=3, reverse=True) - log_keep
        log_a = jnp.where(causal, jax.nn.log_sigmoid(z) + later, -jnp.inf)
        outs.append(jnp.einsum('bhqk,bkhd->bqhd', jnp.exp(log_a).astype(v.dtype), v[:, :kv_len]))
    return jnp.concatenate(outs, axis=1)


def hybrid_mixer(x, positions, w_in, p_a, p_b, w_o):
    B, S, _ = x.shape
    proj = jnp.einsum('bsd,de->bse', x, w_in)
    qa, ka, va, qb, kb, vb, ga, gb = jnp.split(proj, np.cumsum(IN_SPLITS)[:-1], axis=-1)
    shp_a = (B, S, N_DIL_GROUPS, HEADS_PER_DIL_GROUP, HEAD_DIM)
    qa = partial_rotary(qa.reshape(shp_a), positions)
    ka = partial_rotary(ka.reshape(shp_a), positions)
    va = va.reshape(shp_a)
    outs, lses = [], []
    for g, (window, dilation) in enumerate(DIL_CONFIGS):
        o, l = dilated_window_attention(qa[:, :, g], ka[:, :, g], va[:, :, g], window, dilation)
        outs.append(o)
        lses.append(l)
    w_grp = jax.nn.softmax(jnp.stack(lses, axis=0), axis=0)
    o_a = jnp.sum(w_grp[..., None].astype(x.dtype) * jnp.stack(outs, axis=0), axis=0).reshape(B, S, OUT_A)
    shp_b = (B, S, N_SB_HEADS, HEAD_DIM)
    o_b = stick_breaking_attention(qb.reshape(shp_b), kb.reshape(shp_b), vb.reshape(shp_b)).reshape(B, S, WIDTH_B)
    merged = (jax.nn.sigmoid(ga) * jnp.einsum('bse,ed->bsd', o_a, p_a)
              + jax.nn.sigmoid(gb) * jnp.einsum('bse,ed->bsd', o_b, p_b))
    return jnp.einsum('bsd,de->bse', merged, w_o).astype(x.dtype)


def hierarchical_moe(x, w_coarse, b_coarse, w_fine, b_fine, w_up, w_down):
    B, S, D = x.shape
    xt = x.reshape(B * S, D)
    coarse_p = jax.nn.softmax((xt @ w_coarse).astype(jnp.float32) + b_coarse, axis=-1)
    p_grp, g_idx = lax.top_k(coarse_p, 1)
    fine_logits = ((xt @ w_fine).astype(jnp.float32) + b_fine).reshape(-1, N_EXPERT_GROUPS, EXPERTS_PER_GROUP)
    sel = jnp.take_along_axis(fine_logits, g_idx[:, :, None], axis=1)[:, 0]
    top_v, top_i = lax.top_k(sel, TOP_K_FINE)
    w_top = jax.nn.softmax(top_v, axis=-1)
    within = jnp.sum(jax.nn.one_hot(top_i, EXPERTS_PER_GROUP, dtype=jnp.float32) * w_top[..., None], axis=1)
    gates = (jax.nn.one_hot(g_idx[:, 0], N_EXPERT_GROUPS, dtype=jnp.float32)[:, :, None]
             * (p_grp * within)[:, None, :]).reshape(-1, N_EXPERTS)
    y = jnp.zeros((B * S, D), jnp.float32)
    for e in range(N_EXPERTS):
        h = xt @ w_up[e]
        a, b = h[:, :EXPERT_FF], h[:, EXPERT_FF:]
        y = y + gates[:, e:e + 1] * ((jax.nn.silu(a) * b) @ w_down[e]).astype(jnp.float32)
    return y.reshape(B, S, D).astype(x.dtype)


def setup_inputs(seed: int = 0) -> dict:
    key = jax.random.key(seed)
    ks = jax.random.split(key, 16)
    nrm = jax.random.normal
    x = nrm(ks[0], (BATCH, SEQ, D_MODEL), jnp.float32)
    offs = jax.random.randint(ks[1], (BATCH, 1), 0, 4096, dtype=jnp.int32)
    positions = (jnp.arange(SEQ, dtype=jnp.int32)[None, :] + offs).astype(jnp.int32)
    w_in = nrm(ks[2], (DEPTH, D_MODEL, IN_WIDTH), jnp.float32) * D_MODEL ** -0.5
    p_a = nrm(ks[3], (DEPTH, OUT_A, D_MODEL), jnp.float32) * OUT_A ** -0.5
    p_b = nrm(ks[4], (DEPTH, WIDTH_B, D_MODEL), jnp.float32) * WIDTH_B ** -0.5
    w_o = nrm(ks[5], (DEPTH, D_MODEL, D_MODEL), jnp.float32) * (D_MODEL ** -0.5 * DEEPNORM_BETA)
    ln_g = 1.0 + 0.02 * nrm(ks[6], (DEPTH, 2, D_MODEL), jnp.float32)
    ln_b = 0.02 * nrm(ks[7], (DEPTH, 2, D_MODEL), jnp.float32)
    router_coarse = nrm(ks[8], (DEPTH, D_MODEL, N_EXPERT_GROUPS), jnp.float32) * D_MODEL ** -0.5
    router_coarse_bias = 0.01 * nrm(ks[9], (DEPTH, N_EXPERT_GROUPS), jnp.float32)
    router_fine = nrm(ks[10], (DEPTH, D_MODEL, N_EXPERTS), jnp.float32) * D_MODEL ** -0.5
    router_fine_bias = 0.01 * nrm(ks[11], (DEPTH, N_EXPERTS), jnp.float32)
    w_up = nrm(ks[12], (DEPTH, N_EXPERTS, D_MODEL, 2 * EXPERT_FF), jnp.float32) * D_MODEL ** -0.5
    w_down = nrm(ks[13], (DEPTH, N_EXPERTS, EXPERT_FF, D_MODEL), jnp.float32) * (EXPERT_FF ** -0.5 * DEEPNORM_BETA)
    return {"x": x, "positions": positions, "w_in": w_in, "p_a": p_a, "p_b": p_b, "w_o": w_o,
            "ln_g": ln_g, "ln_b": ln_b, "router_coarse": router_coarse,
            "router_coarse_bias": router_coarse_bias, "router_fine": router_fine,
            "router_fine_bias": router_fine_bias, "w_up": w_up, "w_down": w_down}


def reference(x, positions, w_in, p_a, p_b, w_o, ln_g, ln_b, router_coarse, router_coarse_bias,
              router_fine, router_fine_bias, w_up, w_down):
    h = x
    for l in range(DEPTH):
        y = hybrid_mixer(h, positions, w_in[l], p_a[l], p_b[l], w_o[l])
        h = layer_norm(DEEPNORM_ALPHA * h + y, ln_g[l, 0], ln_b[l, 0])
        y = hierarchical_moe(h, router_coarse[l], router_coarse_bias[l], router_fine[l],
                             router_fine_bias[l], w_up[l], w_down[l])
        h = layer_norm(DEEPNORM_ALPHA * h + y, ln_g[l, 1], ln_b[l, 1])
    return h
```

```python
import functools

import jax
import jax.numpy as jnp
from jax import lax
from jax.experimental import pallas as pl
from jax.experimental.pallas import tpu as pltpu

HEAD_DIM = 128
DIL_CONFIGS = ((128, 1), (512, 4), (2048, 16))
N_DIL_GROUPS = 3
HEADS_PER_DIL_GROUP = 4
N_SB_HEADS = 8
ROPE_THETA = 500000.0
ROPE_DIM = HEAD_DIM // 4
N_EXPERT_GROUPS = 4
EXPERTS_PER_GROUP = 4
N_EXPERTS = N_EXPERT_GROUPS * EXPERTS_PER_GROUP
LN_EPS = 1e-5

WIDTH_A = N_DIL_GROUPS * HEADS_PER_DIL_GROUP * HEAD_DIM
GROUP_W = HEADS_PER_DIL_GROUP * HEAD_DIM
WIDTH_B = N_SB_HEADS * HEAD_DIM

BLK = 128
LANES = 128
ROUTER_LANE0 = N_EXPERT_GROUPS
EXPERT_TILE = 256
VMEM_LIMIT = 56 * 1024 * 1024
NEG_BIG = -1e30

f32 = jnp.float32
bf16 = jnp.bfloat16


def _matmul_kernel(x_ref, w_ref, o_ref):
    o_ref[...] = jnp.dot(x_ref[...], w_ref[...], preferred_element_type=f32).astype(o_ref.dtype)


def _matmul(x, w, tm, tn, out_dtype):
    m, k = x.shape
    n = w.shape[1]
    return pl.pallas_call(
        _matmul_kernel,
        grid=(m // tm, n // tn),
        in_specs=[pl.BlockSpec((tm, k), lambda i, j: (i, 0)),
                  pl.BlockSpec((k, tn), lambda i, j: (0, j))],
        out_specs=pl.BlockSpec((tm, tn), lambda i, j: (i, j)),
        out_shape=jax.ShapeDtypeStruct((m, n), out_dtype),
        compiler_params=pltpu.CompilerParams(
            dimension_semantics=("parallel", "parallel"), vmem_limit_bytes=VMEM_LIMIT),
        name="in_proj",
    )(x, w)


def _rotate(t, cos_f, sin_f, lane):
    swapped = jnp.where(lane < ROPE_DIM // 2,
                        pltpu.roll(t, LANES - ROPE_DIM // 2, 1),
                        pltpu.roll(t, ROPE_DIM // 2, 1))
    return t * cos_f + swapped * sin_f


def _dil_kernel(q_ref, kc_ref, kp_ref, vc_ref, vp_ref, cc_ref, sc_ref, cp_ref, sp_ref,
                o_ref, lse_ref):
    n = pl.program_id(2)
    cos_c, sin_c, cos_p, sin_p = cc_ref[0], sc_ref[0], cp_ref[0], sp_ref[0]
    lane = lax.broadcasted_iota(jnp.int32, (BLK, LANES), 1)
    qi = lax.broadcasted_iota(jnp.int32, (BLK, 2 * BLK), 0)
    kj = lax.broadcasted_iota(jnp.int32, (BLK, 2 * BLK), 1)
    first_key = jnp.where(n > 0, 0, BLK)
    valid = (kj >= jnp.maximum(qi, first_key)) & (kj <= qi + BLK)
    scale = HEAD_DIM ** -0.5
    for h in range(HEADS_PER_DIL_GROUP):
        sl = slice(h * HEAD_DIM, (h + 1) * HEAD_DIM)
        q = _rotate(q_ref[0, :, sl].astype(f32), cos_c, sin_c, lane).astype(bf16)
        kc = _rotate(kc_ref[0, :, sl].astype(f32), cos_c, sin_c, lane).astype(bf16)
        kp = _rotate(kp_ref[0, :, sl].astype(f32), cos_p, sin_p, lane).astype(bf16)
        k = jnp.concatenate([kp, kc], axis=0)
        v = jnp.concatenate([vp_ref[0, :, sl], vc_ref[0, :, sl]], axis=0)
        s = lax.dot_general(q, k, (((1,), (1,)), ((), ())), preferred_element_type=f32) * scale
        s = jnp.where(valid, s, NEG_BIG)
        m = jnp.max(s, axis=-1, keepdims=True)
        p = jnp.exp(s - m)
        den = jnp.sum(p, axis=-1, keepdims=True)
        o = jnp.dot(p.astype(bf16), v, preferred_element_type=f32) / den
        o_ref[0, :, sl] = o.astype(o_ref.dtype)
        lse_ref[0, :, sl] = jnp.broadcast_to(m + jnp.log(den), (BLK, HEAD_DIM))


def _dilated_attention(proj, cos_f, sin_f, g, dilation, col):
    b, s, w = proj.shape
    sub = s // dilation
    nb = sub // BLK
    wb = w // GROUP_W
    pv = proj.reshape(b, sub, dilation * w)
    cv = cos_f.reshape(b, sub, dilation * LANES)
    sv = sin_f.reshape(b, sub, dilation * LANES)
    qb, kb, vb = col["qa"] // GROUP_W + g, col["ka"] // GROUP_W + g, col["va"] // GROUP_W + g

    def cur(c):
        return pl.BlockSpec((1, BLK, GROUP_W), lambda bi, r, n: (bi, n, r * wb + c))

    def prev(c):
        return pl.BlockSpec((1, BLK, GROUP_W), lambda bi, r, n: (bi, jnp.maximum(n - 1, 0), r * wb + c))

    tab_cur = pl.BlockSpec((1, BLK, LANES), lambda bi, r, n: (bi, n, r))
    tab_prev = pl.BlockSpec((1, BLK, LANES), lambda bi, r, n: (bi, jnp.maximum(n - 1, 0), r))
    out_spec = pl.BlockSpec((1, BLK, GROUP_W), lambda bi, r, n: (bi, n, r))
    o, lse = pl.pallas_call(
        _dil_kernel,
        grid=(b, dilation, nb),
        in_specs=[cur(qb), cur(kb), prev(kb), cur(vb), prev(vb), tab_cur, tab_cur, tab_prev, tab_prev],
        out_specs=[out_spec, out_spec],
        out_shape=[jax.ShapeDtypeStruct((b, sub, dilation * GROUP_W), bf16),
                   jax.ShapeDtypeStruct((b, sub, dilation * GROUP_W), f32)],
        compiler_params=pltpu.CompilerParams(
            dimension_semantics=("parallel", "parallel", "arbitrary"), vmem_limit_bytes=VMEM_LIMIT),
        name=f"dilated_attn_g{g}",
    )(pv, pv, pv, pv, pv, cv, sv, cv, sv)
    return o.reshape(b * s, GROUP_W), lse.reshape(b * s, GROUP_W)


def _sb_kernel(q_ref, k_ref, v_ref, o_ref):
    i = pl.program_id(2)
    q = q_ref[0]
    scale = HEAD_DIM ** -0.5
    row = lax.broadcasted_iota(jnp.int32, (BLK, BLK), 0)
    coli = lax.broadcasted_iota(jnp.int32, (BLK, BLK), 1)
    causal = coli < row
    jr = lax.broadcasted_iota(jnp.int32, (2 * BLK, BLK), 0)
    sc = lax.broadcasted_iota(jnp.int32, (2 * BLK, BLK), 1)
    upper2 = jnp.where((jr & (BLK - 1)) > sc, 1.0, 0.0).astype(bf16)

    def block(j, carry, diag):
        acc, c = carry
        start = pl.multiple_of(j * BLK, BLK)
        kj = k_ref[0, pl.ds(start, BLK), :]
        vj = v_ref[0, pl.ds(start, BLK), :]
        z = lax.dot_general(q, kj, (((1,), (1,)), ((), ())), preferred_element_type=f32) * scale
        ls = jnp.minimum(z, 0.0) - jnp.log1p(jnp.exp(-jnp.abs(z)))
        lk = ls - z
        if diag:
            lk = jnp.where(causal, lk, 0.0)
        hi = lk.astype(bf16)
        lo = (lk - hi.astype(f32)).astype(bf16)
        later = jnp.dot(jnp.concatenate([hi, lo], axis=1), upper2, preferred_element_type=f32)
        a = jnp.exp(ls + later + c)
        if diag:
            a = jnp.where(causal, a, 0.0)
        acc = acc + jnp.dot(a.astype(bf16), vj, preferred_element_type=f32)
        c = c + jnp.sum(lk, axis=1, keepdims=True)
        return acc, c

    carry = block(i, (jnp.zeros((BLK, HEAD_DIM), f32), jnp.zeros((BLK, 1), f32)), True)
    acc, _ = lax.fori_loop(0, i, lambda t, cr: block(i - 1 - t, cr, False), carry)
    o_ref[0] = acc.astype(o_ref.dtype)


def _stick_breaking(proj, col):
    b, s, w = proj.shape
    qb, kb, vb = col["qb"] // HEAD_DIM, col["kb"] // HEAD_DIM, col["vb"] // HEAD_DIM
    o = pl.pallas_call(
        _sb_kernel,
        grid=(b, N_SB_HEADS, s // BLK),
        in_specs=[pl.BlockSpec((1, BLK, HEAD_DIM), lambda bi, h, i: (bi, i, qb + h)),
                  pl.BlockSpec((1, s, HEAD_DIM), lambda bi, h, i: (bi, 0, kb + h)),
                  pl.BlockSpec((1, s, HEAD_DIM), lambda bi, h, i: (bi, 0, vb + h))],
        out_specs=pl.BlockSpec((1, BLK, HEAD_DIM), lambda bi, h, i: (bi, i, h)),
        out_shape=jax.ShapeDtypeStruct((b, s, WIDTH_B), bf16),
        compiler_params=pltpu.CompilerParams(
            dimension_semantics=("parallel", "parallel", "arbitrary"), vmem_limit_bytes=VMEM_LIMIT),
        name="stick_breaking_attn",
    )(proj, proj, proj)
    return o.reshape(b * s, WIDTH_B)


def _layer_norm(u, g, b):
    mu = jnp.mean(u, axis=-1, keepdims=True)
    d = u - mu
    var = jnp.mean(d * d, axis=-1, keepdims=True)
    return d * lax.rsqrt(var + LN_EPS) * g + b


def _mix_kernel(o0_ref, o1_ref, o2_ref, l0_ref, l1_ref, l2_ref, ob_ref, ga_ref, gb_ref, h_ref,
                pa_ref, pb_ref, wo_ref, lng_ref, lnb_ref, wr_ref, br_ref,
                h1_ref, route_ref, gate_ref, cnt_ref, carry_ref, *, alpha):
    step = pl.program_id(0)

    @pl.when(step == 0)
    def _():
        carry_ref[...] = jnp.zeros_like(carry_ref)

    l0, l1, l2 = l0_ref[...], l1_ref[...], l2_ref[...]
    m = jnp.maximum(jnp.maximum(l0, l1), l2)
    e0, e1, e2 = jnp.exp(l0 - m), jnp.exp(l1 - m), jnp.exp(l2 - m)
    o_a = (e0 * o0_ref[...].astype(f32) + e1 * o1_ref[...].astype(f32)
           + e2 * o2_ref[...].astype(f32)) / (e0 + e1 + e2)
    ya = jnp.dot(o_a.astype(bf16), pa_ref[...], preferred_element_type=f32)
    yb = jnp.dot(ob_ref[...], pb_ref[...], preferred_element_type=f32)
    merged = (jax.nn.sigmoid(ga_ref[...].astype(f32)) * ya
              + jax.nn.sigmoid(gb_ref[...].astype(f32)) * yb)
    y = jnp.dot(merged.astype(bf16), wo_ref[...], preferred_element_type=f32)
    h1 = _layer_norm(alpha * h_ref[...] + y, lng_ref[...], lnb_ref[...])
    h1_ref[...] = h1
    h1b = h1.astype(bf16)

    tm = h1.shape[0]
    logits = jnp.dot(h1b, wr_ref[...], preferred_element_type=f32) + br_ref[...]
    lane = lax.broadcasted_iota(jnp.int32, (tm, LANES), 1).astype(f32)
    no_lane = float(LANES)
    lc = jnp.where(lane < N_EXPERT_GROUPS, logits, NEG_BIG)
    mc = jnp.max(lc, axis=-1, keepdims=True)
    p_grp = 1.0 / jnp.sum(jnp.exp(lc - mc), axis=-1, keepdims=True)
    g_idx = jnp.min(jnp.where(lc == mc, lane, no_lane), axis=-1, keepdims=True)
    lo_lane = ROUTER_LANE0 + EXPERTS_PER_GROUP * g_idx
    sel = (lane >= lo_lane) & (lane < lo_lane + EXPERTS_PER_GROUP)
    lf = jnp.where(sel, logits, NEG_BIG)
    v1 = jnp.max(lf, axis=-1, keepdims=True)
    i1 = jnp.min(jnp.where(lf == v1, lane, no_lane), axis=-1, keepdims=True)
    lf2 = jnp.where(lane == i1, NEG_BIG, lf)
    v2 = jnp.max(lf2, axis=-1, keepdims=True)
    i2 = jnp.min(jnp.where((lf2 == v2) & sel & (lane != i1), lane, no_lane), axis=-1, keepdims=True)
    t = jnp.exp(v2 - v1)
    w1 = 1.0 / (1.0 + t)
    w2 = t / (1.0 + t)
    gate_ref[...] = jnp.where(lane == 0, p_grp * w1, jnp.where(lane == 1, p_grp * w2, 0.0))

    hit1 = lane == i1
    hit2 = lane == i2
    onehot = jnp.where(hit1 | hit2, 1.0, 0.0)
    r = lax.broadcasted_iota(jnp.int32, (tm, tm), 0)
    c = lax.broadcasted_iota(jnp.int32, (tm, tm), 1)
    lower = jnp.where(c < r, 1.0, 0.0).astype(bf16)
    before = jnp.dot(lower, onehot.astype(bf16), preferred_element_type=f32) + carry_ref[0:1, :]
    r1 = jnp.sum(jnp.where(hit1, before, 0.0), axis=-1, keepdims=True)
    r2 = jnp.sum(jnp.where(hit2, before, 0.0), axis=-1, keepdims=True)
    route = jnp.where(lane == 0, i1 - ROUTER_LANE0,
                      jnp.where(lane == 1, i2 - ROUTER_LANE0,
                                jnp.where(lane == 2, r1, jnp.where(lane == 3, r2, 0.0))))
    route_ref[...] = route.astype(jnp.int32)
    new_carry = carry_ref[0:1, :] + jnp.sum(onehot, axis=0, keepdims=True)
    carry_ref[...] = jnp.broadcast_to(new_carry, carry_ref.shape)
    cnt_ref[...] = jnp.broadcast_to(new_carry, cnt_ref.shape)


def _mix(outs, lses, o_b, proj2, h, pa, pb, wo, lng, lnb, wr, br, col, alpha, tm):
    t, d = h.shape
    row = lambda width, c: pl.BlockSpec((tm, width), lambda i: (i, c))
    const = lambda shape: pl.BlockSpec(shape, lambda i: (0, 0), pipeline_mode=pl.Buffered(1))
    in_specs = ([row(GROUP_W, 0)] * 6
                + [row(WIDTH_B, 0), row(d, col["ga"] // d), row(d, col["gb"] // d), row(d, 0),
                   const(pa.shape), const(pb.shape), const(wo.shape),
                   const(lng.shape), const(lnb.shape), const(wr.shape), const(br.shape)])
    return pl.pallas_call(
        functools.partial(_mix_kernel, alpha=alpha),
        grid=(t // tm,),
        in_specs=in_specs,
        out_specs=[row(d, 0), row(LANES, 0), row(LANES, 0),
                   pl.BlockSpec((8, LANES), lambda i: (0, 0))],
        out_shape=[jax.ShapeDtypeStruct((t, d), f32),
                   jax.ShapeDtypeStruct((t, LANES), jnp.int32), jax.ShapeDtypeStruct((t, LANES), f32),
                   jax.ShapeDtypeStruct((8, LANES), f32)],
        scratch_shapes=[pltpu.VMEM((8, LANES), f32)],
        compiler_params=pltpu.CompilerParams(
            dimension_semantics=("arbitrary",), vmem_limit_bytes=VMEM_LIMIT),
        name="mix_ln_router",
    )(*outs, *lses, o_b, proj2, proj2, h, pa, pb, wo, lng, lnb, wr, br)


def _gather_rows(idx_ref, n_rows, src_hbm, dst, sem):
    def body(r, _):
        pltpu.make_async_copy(src_hbm.at[pl.ds(idx_ref[0, 0, r], 1)], dst.at[pl.ds(r, 1)], sem).start()
        return 0
    lax.fori_loop(0, n_rows, body, 0, unroll=8)


def _expert_kernel(te_ref, nu_ref, src_cur, src_nxt, x_hbm, wu_ref, wd_ref, o_ref, xbuf, sem):
    i = pl.program_id(0)
    n = pl.num_programs(0)
    slot = i % 2
    ff = wd_ref.shape[1]

    @pl.when(i == 0)
    def _():
        _gather_rows(src_cur, EXPERT_TILE, x_hbm, xbuf.at[0], sem.at[0])

    @pl.when(i + 1 < n)
    def _():
        _gather_rows(src_nxt, EXPERT_TILE, x_hbm, xbuf.at[1 - slot], sem.at[1 - slot])

    pltpu.make_async_copy(x_hbm.at[pl.ds(0, EXPERT_TILE)], xbuf.at[slot], sem.at[slot]).wait()

    @pl.when(i < nu_ref[0])
    def _():
        x = xbuf[slot].astype(bf16)
        hh = jnp.dot(x, wu_ref[0], preferred_element_type=f32)
        a, b = hh[:, :ff], hh[:, ff:]
        act = a * jax.nn.sigmoid(a) * b
        o_ref[...] = jnp.dot(act.astype(bf16), wd_ref[0], preferred_element_type=f32)

    @pl.when(i >= nu_ref[0])
    def _():
        o_ref[...] = jnp.zeros_like(o_ref)


def _experts(h1, src, tile_expert, n_used, wu, wd):
    t, d = h1.shape
    n_tiles = src.shape[0]
    ff2 = wu.shape[2]
    ff = wd.shape[1]
    grid_spec = pltpu.PrefetchScalarGridSpec(
        num_scalar_prefetch=2,
        grid=(n_tiles,),
        in_specs=[
            pl.BlockSpec((1, 1, EXPERT_TILE), lambda i, te, nu: (i, 0, 0), memory_space=pltpu.SMEM),
            pl.BlockSpec((1, 1, EXPERT_TILE), lambda i, te, nu: (jnp.minimum(i + 1, n_tiles - 1), 0, 0),
                         memory_space=pltpu.SMEM),
            pl.BlockSpec(memory_space=pl.ANY),
            pl.BlockSpec((1, d, ff2), lambda i, te, nu: (te[i], 0, 0)),
            pl.BlockSpec((1, ff, d), lambda i, te, nu: (te[i], 0, 0)),
        ],
        out_specs=pl.BlockSpec((EXPERT_TILE, d), lambda i, te, nu: (i, 0)),
        scratch_shapes=[pltpu.VMEM((2, EXPERT_TILE, d), f32), pltpu.SemaphoreType.DMA((2,))],
    )
    return pl.pallas_call(
        _expert_kernel,
        grid_spec=grid_spec,
        out_shape=jax.ShapeDtypeStruct((n_tiles * EXPERT_TILE, d), f32),
        compiler_params=pltpu.CompilerParams(
            dimension_semantics=("arbitrary",), vmem_limit_bytes=VMEM_LIMIT),
        name="experts",
    )(tile_expert, n_used, src, src, h1, wu, wd)


def _combine_kernel(pos_cur, pos_nxt, ys_hbm, h1_ref, gate_ref, lng_ref, lnb_ref,
                    h2_ref, h2b_ref, ybuf, sem, *, alpha, tm):
    i = pl.program_id(0)
    n = pl.num_programs(0)
    slot = i % 2

    @pl.when(i == 0)
    def _():
        _gather_rows(pos_cur, 2 * tm, ys_hbm, ybuf.at[0], sem.at[0])

    @pl.when(i + 1 < n)
    def _():
        _gather_rows(pos_nxt, 2 * tm, ys_hbm, ybuf.at[1 - slot], sem.at[1 - slot])

    pltpu.make_async_copy(ys_hbm.at[pl.ds(0, 2 * tm)], ybuf.at[slot], sem.at[slot]).wait()
    g = gate_ref[...]
    y = g[:, 0:1] * ybuf[slot, 0:tm, :] + g[:, 1:2] * ybuf[slot, tm:2 * tm, :]
    h2 = _layer_norm(alpha * h1_ref[...] + y, lng_ref[...], lnb_ref[...])
    h2_ref[...] = h2
    h2b_ref[...] = h2.astype(bf16)


def _combine(pos, ys, h1, gate, lng, lnb, alpha, tm):
    t, d = h1.shape
    n = t // tm
    row = lambda width: pl.BlockSpec((tm, width), lambda i: (i, 0))
    const = lambda shape: pl.BlockSpec(shape, lambda i: (0, 0))
    return pl.pallas_call(
        functools.partial(_combine_kernel, alpha=alpha, tm=tm),
        grid=(n,),
        in_specs=[
            pl.BlockSpec((1, 1, 2 * tm), lambda i: (i, 0, 0), memory_space=pltpu.SMEM),
            pl.BlockSpec((1, 1, 2 * tm), lambda i: (jnp.minimum(i + 1, n - 1), 0, 0),
                         memory_space=pltpu.SMEM),
            pl.BlockSpec(memory_space=pl.ANY),
            row(d), row(LANES), const(lng.shape), const(lnb.shape)],
        out_specs=[row(d), row(d)],
        out_shape=[jax.ShapeDtypeStruct((t, d), f32), jax.ShapeDtypeStruct((t, d), bf16)],
        scratch_shapes=[pltpu.VMEM((2, 2 * tm, d), f32), pltpu.SemaphoreType.DMA((2,))],
        compiler_params=pltpu.CompilerParams(
            dimension_semantics=("arbitrary",), vmem_limit_bytes=VMEM_LIMIT),
        name="combine_ln",
    )(pos, pos, ys, h1, gate, lng, lnb)


def _rope_tables(positions):
    half = ROPE_DIM // 2
    inv_freq = ROPE_THETA ** (-2.0 * jnp.arange(half, dtype=f32) / ROPE_DIM)
    ang = positions.astype(f32)[:, :, None] * inv_freq
    cos, sin = jnp.cos(ang), jnp.sin(ang)
    pad = LANES - ROPE_DIM
    cos_f = jnp.concatenate([cos, cos, jnp.ones(cos.shape[:2] + (pad,), f32)], axis=-1)
    sin_f = jnp.concatenate([-sin, sin, jnp.zeros(sin.shape[:2] + (pad,), f32)], axis=-1)
    return cos_f, sin_f


def _route_plan(route, cnt, t, n_tiles):
    ids = route[:, 0:2]
    ranks = route[:, 2:4]
    counts = cnt[0, ROUTER_LANE0:ROUTER_LANE0 + N_EXPERTS].astype(jnp.int32)
    padded = (counts + EXPERT_TILE - 1) // EXPERT_TILE * EXPERT_TILE
    ends = jnp.cumsum(padded)
    offs = ends - padded
    pos = offs[ids] + ranks
    n_used = (ends[-1] // EXPERT_TILE).astype(jnp.int32)
    tile_start = jnp.arange(n_tiles, dtype=jnp.int32) * EXPERT_TILE
    tile_expert = jnp.minimum(jnp.searchsorted(ends, tile_start, side="right"),
                              N_EXPERTS - 1).astype(jnp.int32)
    last_used = tile_expert[jnp.maximum(n_used - 1, 0)]
    tile_expert = jnp.where(jnp.arange(n_tiles) < n_used, tile_expert, last_used)
    tok = jnp.repeat(jnp.arange(t, dtype=jnp.int32), 2)
    src = jnp.zeros((n_tiles * EXPERT_TILE,), jnp.int32).at[pos.reshape(-1)].set(tok)
    return pos, src.reshape(n_tiles, 1, EXPERT_TILE), tile_expert, n_used.reshape(1)


def kernel(x, positions, w_in, p_a, p_b, w_o, ln_g, ln_b, router_coarse, router_coarse_bias,
           router_fine, router_fine_bias, w_up, w_down):
    b, s, d = x.shape
    depth = w_in.shape[0]
    t = b * s
    alpha = float((2 * depth) ** 0.25)
    assert s % (DIL_CONFIGS[-1][1] * BLK) == 0 and d % GROUP_W == 0

    sizes = (("ga", d), ("gb", d), ("qa", WIDTH_A), ("ka", WIDTH_A), ("va", WIDTH_A),
             ("qb", WIDTH_B), ("kb", WIDTH_B), ("vb", WIDTH_B))
    col, off = {}, 0
    for name, width in sizes:
        col[name] = off
        off += width
    w_total = off
    a3, b3 = 3 * WIDTH_A, 3 * WIDTH_B

    cos_f, sin_f = _rope_tables(positions)
    tm_proj = min(1024, t)
    tm_mix = min(256, t)
    tm_comb = min(256, t)
    n_tiles = (2 * t) // EXPERT_TILE + N_EXPERTS

    h = x.reshape(t, d)
    hb = h.astype(bf16)
    for l in range(depth):
        w = w_in[l]
        w_perm = jnp.concatenate([w[:, a3 + b3:], w[:, :a3 + b3]], axis=1).astype(bf16)
        proj = _matmul(hb, w_perm, tm_proj, GROUP_W, bf16)
        proj3 = proj.reshape(b, s, w_total)
        outs, lses = [], []
        for g, (_, dilation) in enumerate(DIL_CONFIGS):
            o, lse = _dilated_attention(proj3, cos_f, sin_f, g, dilation, col)
            outs.append(o)
            lses.append(lse)
        o_b = _stick_breaking(proj3, col)

        wr = jnp.zeros((d, LANES), f32)
        wr = wr.at[:, :N_EXPERT_GROUPS].set(router_coarse[l])
        wr = wr.at[:, ROUTER_LANE0:ROUTER_LANE0 + N_EXPERTS].set(router_fine[l]).astype(bf16)
        br = jnp.zeros((1, LANES), f32)
        br = br.at[0, :N_EXPERT_GROUPS].set(router_coarse_bias[l])
        br = br.at[0, ROUTER_LANE0:ROUTER_LANE0 + N_EXPERTS].set(router_fine_bias[l])
        h1, route, gate, cnt = _mix(
            outs, lses, o_b, proj, h, p_a[l].astype(bf16), p_b[l].astype(bf16), w_o[l].astype(bf16),
            ln_g[l, 0].reshape(1, d), ln_b[l, 0].reshape(1, d), wr, br, col, alpha, tm_mix)

        pos, src, tile_expert, n_used = _route_plan(route, cnt, t, n_tiles)
        ys = _experts(h1, src, tile_expert, n_used, w_up[l].astype(bf16), w_down[l].astype(bf16))
        pos_tiles = pos.reshape(t // tm_comb, tm_comb, 2).transpose(0, 2, 1).reshape(t // tm_comb, 1, 2 * tm_comb)
        h, hb = _combine(pos_tiles, ys, h1, gate, ln_g[l, 1].reshape(1, d), ln_b[l, 1].reshape(1, d),
                         alpha, tm_comb)
    return h.reshape(b, s, d)
```

```python
import functools

import jax
import jax.numpy as jnp
from jax import lax
from jax.experimental import pallas as pl
from jax.experimental.pallas import tpu as pltpu

HEAD_DIM = 128
DIL_CONFIGS = ((128, 1), (512, 4), (2048, 16))
N_DIL_GROUPS = 3
HEADS_PER_DIL_GROUP = 4
N_SB_HEADS = 8
ROPE_THETA = 500000.0
ROPE_DIM = HEAD_DIM // 4
N_EXPERT_GROUPS = 4
EXPERTS_PER_GROUP = 4
N_EXPERTS = N_EXPERT_GROUPS * EXPERTS_PER_GROUP
LN_EPS = 1e-5

WIDTH_A = N_DIL_GROUPS * HEADS_PER_DIL_GROUP * HEAD_DIM
GROUP_W = HEADS_PER_DIL_GROUP * HEAD_DIM
WIDTH_B = N_SB_HEADS * HEAD_DIM

BLK = 128
LANES = 128
DIL_RES_PER_STEP = 4
SB_Q = 256
SB_K = 256
SB_HEADS_PER_STEP = 2
ROUTER_LANE0 = N_EXPERT_GROUPS
EXPERT_TILE = 256
VMEM_LIMIT = 56 * 1024 * 1024
NEG_BIG = -1e30

f32 = jnp.float32
bf16 = jnp.bfloat16


def _in_proj_kernel(x_ref, w_ref, o_ref, *scratch, dilation):
    acc = jnp.dot(x_ref[...], w_ref[...], preferred_element_type=f32)
    if dilation == 1:
        o_ref[0, 0] = acc.astype(o_ref.dtype)
        return
    (acc_ref,) = scratch
    n = acc.shape[0] // dilation
    for s in range(acc.shape[1] // LANES):
        acc_ref[s] = acc[:, s * LANES:(s + 1) * LANES]
    for r in range(dilation):
        for s in range(acc.shape[1] // LANES):
            o_ref[0, r, :, s * LANES:(s + 1) * LANES] = (
                acc_ref[s, pl.ds(r, n, stride=dilation), :].astype(o_ref.dtype))


def _in_proj(x, w, b, s, dilation, tm, tn):
    k = x.shape[1]
    n = w.shape[1]
    per_b = s // tm
    scratch = [] if dilation == 1 else [pltpu.VMEM((tn // LANES, tm, LANES), f32)]
    return pl.pallas_call(
        functools.partial(_in_proj_kernel, dilation=dilation),
        grid=(b * per_b, n // tn),
        in_specs=[pl.BlockSpec((tm, k), lambda i, j: (i, 0)),
                  pl.BlockSpec((k, tn), lambda i, j: (0, j))],
        out_specs=pl.BlockSpec((1, dilation, tm // dilation, tn),
                               lambda i, j: (i // per_b, 0, i % per_b, j)),
        out_shape=jax.ShapeDtypeStruct((b, dilation, s // dilation, n), bf16),
        scratch_shapes=scratch,
        compiler_params=pltpu.CompilerParams(
            dimension_semantics=("parallel", "parallel"), vmem_limit_bytes=VMEM_LIMIT),
        name=f"in_proj_d{dilation}",
    )(x, w)


def _rotate(t, cos_f, sin_f, lane):
    swapped = jnp.where(lane < ROPE_DIM // 2,
                        pltpu.roll(t, LANES - ROPE_DIM // 2, 1),
                        pltpu.roll(t, ROPE_DIM // 2, 1))
    return t * cos_f + swapped * sin_f


def _dil_kernel(q_ref, kc_ref, kp_ref, vc_ref, vp_ref, cc_ref, sc_ref, cp_ref, sp_ref,
                o_ref, lse_ref, o_scr, lse_scr, *, dilation, res_per_step):
    n = pl.program_id(1)
    rr = pl.program_id(2)
    lane = lax.broadcasted_iota(jnp.int32, (BLK, LANES), 1)
    qi = lax.broadcasted_iota(jnp.int32, (BLK, 2 * BLK), 0)
    kj = lax.broadcasted_iota(jnp.int32, (BLK, 2 * BLK), 1)
    first_key = jnp.where(n > 0, 0, BLK)
    valid = (kj >= jnp.maximum(qi, first_key)) & (kj <= qi + BLK)
    scale = HEAD_DIM ** -0.5

    def residue(ri, _):
        cos_c, sin_c, cos_p, sin_p = cc_ref[0, ri], sc_ref[0, ri], cp_ref[0, ri], sp_ref[0, ri]
        rows = pl.ds(rr * res_per_step + ri, BLK, stride=dilation) if dilation > 1 else slice(None)
        for h in range(HEADS_PER_DIL_GROUP):
            sl = slice(h * HEAD_DIM, (h + 1) * HEAD_DIM)
            q = _rotate(q_ref[0, ri, :, sl].astype(f32), cos_c, sin_c, lane).astype(bf16)
            kc = _rotate(kc_ref[0, ri, :, sl].astype(f32), cos_c, sin_c, lane).astype(bf16)
            kp = _rotate(kp_ref[0, ri, :, sl].astype(f32), cos_p, sin_p, lane).astype(bf16)
            k = jnp.concatenate([kp, kc], axis=0)
            v = jnp.concatenate([vp_ref[0, ri, :, sl], vc_ref[0, ri, :, sl]], axis=0)
            s = lax.dot_general(q, k, (((1,), (1,)), ((), ())), preferred_element_type=f32) * scale
            s = jnp.where(valid, s, NEG_BIG)
            m = jnp.max(s, axis=-1, keepdims=True)
            p = jnp.exp(s - m)
            den = jnp.sum(p, axis=-1, keepdims=True)
            o_scr[h, rows, :] = jnp.dot(p.astype(bf16), v, preferred_element_type=f32) / den
            lse_scr[h, rows, :] = jnp.broadcast_to(m + jnp.log(den), (BLK, HEAD_DIM))
        return 0

    lax.fori_loop(0, res_per_step, residue, 0)

    @pl.when(rr == pl.num_programs(2) - 1)
    def _():
        for h in range(HEADS_PER_DIL_GROUP):
            sl = slice(h * HEAD_DIM, (h + 1) * HEAD_DIM)
            o_ref[0, :, sl] = o_scr[h].astype(o_ref.dtype)
            lse_ref[0, :, sl] = lse_scr[h]


def _dilated_attention(qkv, cos_r, sin_r, dilation):
    b, _, sub, _ = qkv.shape
    s = sub * dilation
    nb = sub // BLK
    rs = min(dilation, DIL_RES_PER_STEP)

    def cur(c):
        return pl.BlockSpec((1, rs, BLK, GROUP_W), lambda bi, n, rr: (bi, rr, n, c))

    def prev(c):
        return pl.BlockSpec((1, rs, BLK, GROUP_W), lambda bi, n, rr: (bi, rr, jnp.maximum(n - 1, 0), c))

    tab_cur = pl.BlockSpec((1, rs, BLK, LANES), lambda bi, n, rr: (bi, rr, n, 0))
    tab_prev = pl.BlockSpec((1, rs, BLK, LANES), lambda bi, n, rr: (bi, rr, jnp.maximum(n - 1, 0), 0))
    out_spec = pl.BlockSpec((1, dilation * BLK, GROUP_W), lambda bi, n, rr: (bi, n, 0))
    o, lse = pl.pallas_call(
        functools.partial(_dil_kernel, dilation=dilation, res_per_step=rs),
        grid=(b, nb, dilation // rs),
        in_specs=[cur(0), cur(1), prev(1), cur(2), prev(2), tab_cur, tab_cur, tab_prev, tab_prev],
        out_specs=[out_spec, out_spec],
        out_shape=[jax.ShapeDtypeStruct((b, s, GROUP_W), bf16),
                   jax.ShapeDtypeStruct((b, s, GROUP_W), f32)],
        scratch_shapes=[pltpu.VMEM((HEADS_PER_DIL_GROUP, dilation * BLK, HEAD_DIM), f32),
                        pltpu.VMEM((HEADS_PER_DIL_GROUP, dilation * BLK, HEAD_DIM), f32)],
        compiler_params=pltpu.CompilerParams(
            dimension_semantics=("parallel", "arbitrary", "arbitrary"), vmem_limit_bytes=VMEM_LIMIT),
        name=f"dilated_attn_d{dilation}",
    )(qkv, qkv, qkv, qkv, qkv, cos_r, sin_r, cos_r, sin_r)
    return o.reshape(b * s, GROUP_W), lse.reshape(b * s, GROUP_W)


def _sb_kernel(q_ref, k_ref, v_ref, up_ref, o_ref, acc_ref, c_ref):
    i = pl.program_id(2)
    scale = HEAD_DIM ** -0.5
    row = lax.broadcasted_iota(jnp.int32, (SB_Q, SB_K), 0)
    coli = lax.broadcasted_iota(jnp.int32, (SB_Q, SB_K), 1)
    causal = coli < row
    acc_ref[...] = jnp.zeros_like(acc_ref)
    c_ref[...] = jnp.zeros_like(c_ref)

    def chunk(j, diag):
        start = pl.multiple_of(j * SB_K, SB_K)
        for hd in range(SB_HEADS_PER_STEP):
            sl = slice(hd * HEAD_DIM, (hd + 1) * HEAD_DIM)
            q = q_ref[0, :, sl]
            kj = k_ref[0, pl.ds(start, SB_K), sl]
            vj = v_ref[0, pl.ds(start, SB_K), sl]
            z = lax.dot_general(q, kj, (((1,), (1,)), ((), ())), preferred_element_type=f32) * scale
            ls = jnp.minimum(z, 0.0) - jnp.log(1.0 + jnp.exp(-jnp.abs(z)))
            lk = ls - z
            if diag:
                lk = jnp.where(causal, lk, 0.0)
            hi = lk.astype(bf16)
            lo = (lk - hi.astype(f32)).astype(bf16)
            later = jnp.dot(jnp.concatenate([hi, lo], axis=1), up_ref[...], preferred_element_type=f32)
            c = c_ref[hd]
            a = jnp.exp(ls + later + jnp.concatenate([c, c], axis=1))
            if diag:
                a = jnp.where(causal, a, 0.0)
            acc_ref[hd] += jnp.dot(a.astype(bf16), vj, preferred_element_type=f32)
            c_ref[hd] = c + jnp.sum(lk, axis=1, keepdims=True)

    chunk(i, True)

    def body(t, _):
        chunk(i - 1 - t, False)
        return 0

    lax.fori_loop(0, i, body, 0)
    for hd in range(SB_HEADS_PER_STEP):
        o_ref[0, :, hd * HEAD_DIM:(hd + 1) * HEAD_DIM] = acc_ref[hd].astype(o_ref.dtype)


def _stick_breaking(main3, col):
    b, s, _ = main3.shape
    w = SB_HEADS_PER_STEP * HEAD_DIM
    qb, kb, vb = col["qb"] // w, col["kb"] // w, col["vb"] // w
    jr = lax.broadcasted_iota(jnp.int32, (2 * SB_K, SB_K), 0)
    sc = lax.broadcasted_iota(jnp.int32, (2 * SB_K, SB_K), 1)
    upper2 = jnp.where((jr % SB_K) > sc, 1.0, 0.0).astype(bf16)
    o = pl.pallas_call(
        _sb_kernel,
        grid=(b, N_SB_HEADS // SB_HEADS_PER_STEP, s // SB_Q),
        in_specs=[pl.BlockSpec((1, SB_Q, w), lambda bi, h, i: (bi, i, qb + h)),
                  pl.BlockSpec((1, s, w), lambda bi, h, i: (bi, 0, kb + h)),
                  pl.BlockSpec((1, s, w), lambda bi, h, i: (bi, 0, vb + h)),
                  pl.BlockSpec((2 * SB_K, SB_K), lambda bi, h, i: (0, 0))],
        out_specs=pl.BlockSpec((1, SB_Q, w), lambda bi, h, i: (bi, i, h)),
        out_shape=jax.ShapeDtypeStruct((b, s, WIDTH_B), bf16),
        scratch_shapes=[pltpu.VMEM((SB_HEADS_PER_STEP, SB_Q, HEAD_DIM), f32),
                        pltpu.VMEM((SB_HEADS_PER_STEP, SB_Q, HEAD_DIM), f32)],
        compiler_params=pltpu.CompilerParams(
            dimension_semantics=("parallel", "parallel", "arbitrary"), vmem_limit_bytes=VMEM_LIMIT),
        name="stick_breaking_attn",
    )(main3, main3, main3, upper2)
    return o.reshape(b * s, WIDTH_B)


def _layer_norm(u, g, b):
    mu = jnp.mean(u, axis=-1, keepdims=True)
    d = u - mu
    var = jnp.mean(d * d, axis=-1, keepdims=True)
    return d * lax.rsqrt(var + LN_EPS) * g + b


def _mix_kernel(o0_ref, o1_ref, o2_ref, l0_ref, l1_ref, l2_ref, ob_ref, ga_ref, gb_ref, h_ref,
                pa_ref, pb_ref, wo_ref, lng_ref, lnb_ref, wr_ref, br_ref,
                h1_ref, route_ref, gate_ref, cnt_ref, carry_ref, *, alpha):
    step = pl.program_id(0)

    @pl.when(step == 0)
    def _():
        carry_ref[...] = jnp.zeros_like(carry_ref)

    l0, l1, l2 = l0_ref[...], l1_ref[...], l2_ref[...]
    m = jnp.maximum(jnp.maximum(l0, l1), l2)
    e0, e1, e2 = jnp.exp(l0 - m), jnp.exp(l1 - m), jnp.exp(l2 - m)
    o_a = (e0 * o0_ref[...].astype(f32) + e1 * o1_ref[...].astype(f32)
           + e2 * o2_ref[...].astype(f32)) / (e0 + e1 + e2)
    ya = jnp.dot(o_a.astype(bf16), pa_ref[...], preferred_element_type=f32)
    yb = jnp.dot(ob_ref[...], pb_ref[...], preferred_element_type=f32)
    merged = (jax.nn.sigmoid(ga_ref[...].astype(f32)) * ya
              + jax.nn.sigmoid(gb_ref[...].astype(f32)) * yb)
    y = jnp.dot(merged.astype(bf16), wo_ref[...], preferred_element_type=f32)
    h1 = _layer_norm(alpha * h_ref[...] + y, lng_ref[...], lnb_ref[...])
    h1_ref[...] = h1
    h1b = h1.astype(bf16)

    tm = h1.shape[0]
    logits = jnp.dot(h1b, wr_ref[...], preferred_element_type=f32) + br_ref[...]
    lane = lax.broadcasted_iota(jnp.int32, (tm, LANES), 1).astype(f32)
    no_lane = float(LANES)
    lc = jnp.where(lane < N_EXPERT_GROUPS, logits, NEG_BIG)
    mc = jnp.max(lc, axis=-1, keepdims=True)
    p_grp = 1.0 / jnp.sum(jnp.exp(lc - mc), axis=-1, keepdims=True)
    g_idx = jnp.min(jnp.where(lc == mc, lane, no_lane), axis=-1, keepdims=True)
    lo_lane = ROUTER_LANE0 + EXPERTS_PER_GROUP * g_idx
    sel = (lane >= lo_lane) & (lane < lo_lane + EXPERTS_PER_GROUP)
    lf = jnp.where(sel, logits, NEG_BIG)
    v1 = jnp.max(lf, axis=-1, keepdims=True)
    i1 = jnp.min(jnp.where(lf == v1, lane, no_lane), axis=-1, keepdims=True)
    lf2 = jnp.where(lane == i1, NEG_BIG, lf)
    v2 = jnp.max(lf2, axis=-1, keepdims=True)
    i2 = jnp.min(jnp.where((lf2 == v2) & sel & (lane != i1), lane, no_lane), axis=-1, keepdims=True)
    t = jnp.exp(v2 - v1)
    w1 = 1.0 / (1.0 + t)
    w2 = t / (1.0 + t)
    gate_ref[...] = jnp.where(lane == 0, p_grp * w1, jnp.where(lane == 1, p_grp * w2, 0.0))

    hit1 = lane == i1
    hit2 = lane == i2
    onehot = jnp.where(hit1 | hit2, 1.0, 0.0)
    r = lax.broadcasted_iota(jnp.int32, (tm, tm), 0)
    c = lax.broadcasted_iota(jnp.int32, (tm, tm), 1)
    lower = jnp.where(c < r, 1.0, 0.0).astype(bf16)
    before = jnp.dot(lower, onehot.astype(bf16), preferred_element_type=f32) + carry_ref[0:1, :]
    r1 = jnp.sum(jnp.where(hit1, before, 0.0), axis=-1, keepdims=True)
    r2 = jnp.sum(jnp.where(hit2, before, 0.0), axis=-1, keepdims=True)
    route = jnp.where(lane == 0, i1 - ROUTER_LANE0,
                      jnp.where(lane == 1, i2 - ROUTER_LANE0,
                                jnp.where(lane == 2, r1, jnp.where(lane == 3, r2, 0.0))))
    route_ref[...] = route.astype(jnp.int32)
    new_carry = carry_ref[0:1, :] + jnp.sum(onehot, axis=0, keepdims=True)
    carry_ref[...] = jnp.broadcast_to(new_carry, carry_ref.shape)
    cnt_ref[...] = jnp.broadcast_to(new_carry, cnt_ref.shape)


def _mix(outs, lses, o_b, main2, h, pa, pb, wo, lng, lnb, wr, br, col, alpha, tm):
    t, d = h.shape
    row = lambda width, c: pl.BlockSpec((tm, width), lambda i: (i, c))
    const = lambda shape: pl.BlockSpec(shape, lambda i: (0, 0), pipeline_mode=pl.Buffered(1))
    in_specs = ([row(GROUP_W, 0)] * 6
                + [row(WIDTH_B, 0), row(d, col["ga"] // d), row(d, col["gb"] // d), row(d, 0),
                   const(pa.shape), const(pb.shape), const(wo.shape),
                   const(lng.shape), const(lnb.shape), const(wr.shape), const(br.shape)])
    return pl.pallas_call(
        functools.partial(_mix_kernel, alpha=alpha),
        grid=(t // tm,),
        in_specs=in_specs,
        out_specs=[row(d, 0), row(LANES, 0), row(LANES, 0),
                   pl.BlockSpec((8, LANES), lambda i: (0, 0))],
        out_shape=[jax.ShapeDtypeStruct((t, d), f32),
                   jax.ShapeDtypeStruct((t, LANES), jnp.int32), jax.ShapeDtypeStruct((t, LANES), f32),
                   jax.ShapeDtypeStruct((8, LANES), f32)],
        scratch_shapes=[pltpu.VMEM((8, LANES), f32)],
        compiler_params=pltpu.CompilerParams(
            dimension_semantics=("arbitrary",), vmem_limit_bytes=VMEM_LIMIT),
        name="mix_ln_router",
    )(*outs, *lses, o_b, main2, main2, h, pa, pb, wo, lng, lnb, wr, br)


def _gather_rows(idx_ref, n_rows, src_hbm, dst, sem):
    def body(r, _):
        pltpu.make_async_copy(src_hbm.at[pl.ds(idx_ref[0, 0, r], 1)], dst.at[pl.ds(r, 1)], sem).start()
        return 0
    lax.fori_loop(0, n_rows, body, 0, unroll=8)


def _expert_kernel(te_ref, nu_ref, src_cur, src_nxt, x_hbm, wu_ref, wd_ref, o_ref, xbuf, sem):
    i = pl.program_id(0)
    n = pl.num_programs(0)
    slot = i % 2
    ff = wd_ref.shape[1]

    @pl.when(i == 0)
    def _():
        _gather_rows(src_cur, EXPERT_TILE, x_hbm, xbuf.at[0], sem.at[0])

    @pl.when(i + 1 < n)
    def _():
        _gather_rows(src_nxt, EXPERT_TILE, x_hbm, xbuf.at[1 - slot], sem.at[1 - slot])

    pltpu.make_async_copy(x_hbm.at[pl.ds(0, EXPERT_TILE)], xbuf.at[slot], sem.at[slot]).wait()

    @pl.when(i < nu_ref[0])
    def _():
        x = xbuf[slot].astype(bf16)
        hh = jnp.dot(x, wu_ref[0], preferred_element_type=f32)
        a, b = hh[:, :ff], hh[:, ff:]
        act = a * jax.nn.sigmoid(a) * b
        o_ref[...] = jnp.dot(act.astype(bf16), wd_ref[0], preferred_element_type=f32)

    @pl.when(i >= nu_ref[0])
    def _():
        o_ref[...] = jnp.zeros_like(o_ref)


def _experts(h1, src, tile_expert, n_used, wu, wd):
    t, d = h1.shape
    n_tiles = src.shape[0]
    ff2 = wu.shape[2]
    ff = wd.shape[1]
    grid_spec = pltpu.PrefetchScalarGridSpec(
        num_scalar_prefetch=2,
        grid=(n_tiles,),
        in_specs=[
            pl.BlockSpec((1, 1, EXPERT_TILE), lambda i, te, nu: (i, 0, 0), memory_space=pltpu.SMEM),
            pl.BlockSpec((1, 1, EXPERT_TILE), lambda i, te, nu: (jnp.minimum(i + 1, n_tiles - 1), 0, 0),
                         memory_space=pltpu.SMEM),
            pl.BlockSpec(memory_space=pl.ANY),
            pl.BlockSpec((1, d, ff2), lambda i, te, nu: (te[i], 0, 0)),
            pl.BlockSpec((1, ff, d), lambda i, te, nu: (te[i], 0, 0)),
        ],
        out_specs=pl.BlockSpec((EXPERT_TILE, d), lambda i, te, nu: (i, 0)),
        scratch_shapes=[pltpu.VMEM((2, EXPERT_TILE, d), f32), pltpu.SemaphoreType.DMA((2,))],
    )
    return pl.pallas_call(
        _expert_kernel,
        grid_spec=grid_spec,
        out_shape=jax.ShapeDtypeStruct((n_tiles * EXPERT_TILE, d), f32),
        compiler_params=pltpu.CompilerParams(
            dimension_semantics=("arbitrary",), vmem_limit_bytes=VMEM_LIMIT),
        name="experts",
    )(tile_expert, n_used, src, src, h1, wu, wd)


def _combine_kernel(pos_cur, pos_nxt, ys_hbm, h1_ref, gate_ref, lng_ref, lnb_ref,
                    h2_ref, h2b_ref, ybuf, sem, *, alpha, tm):
    i = pl.program_id(0)
    n = pl.num_programs(0)
    slot = i % 2

    @pl.when(i == 0)
    def _():
        _gather_rows(pos_cur, 2 * tm, ys_hbm, ybuf.at[0], sem.at[0])

    @pl.when(i + 1 < n)
    def _():
        _gather_rows(pos_nxt, 2 * tm, ys_hbm, ybuf.at[1 - slot], sem.at[1 - slot])

    pltpu.make_async_copy(ys_hbm.at[pl.ds(0, 2 * tm)], ybuf.at[slot], sem.at[slot]).wait()
    g = gate_ref[...]
    y = g[:, 0:1] * ybuf[slot, 0:tm, :] + g[:, 1:2] * ybuf[slot, tm:2 * tm, :]
    h2 = _layer_norm(alpha * h1_ref[...] + y, lng_ref[...], lnb_ref[...])
    h2_ref[...] = h2
    h2b_ref[...] = h2.astype(bf16)


def _combine(pos, ys, h1, gate, lng, lnb, alpha, tm):
    t, d = h1.shape
    n = t // tm
    row = lambda width: pl.BlockSpec((tm, width), lambda i: (i, 0))
    const = lambda shape: pl.BlockSpec(shape, lambda i: (0, 0))
    return pl.pallas_call(
        functools.partial(_combine_kernel, alpha=alpha, tm=tm),
        grid=(n,),
        in_specs=[
            pl.BlockSpec((1, 1, 2 * tm), lambda i: (i, 0, 0), memory_space=pltpu.SMEM),
            pl.BlockSpec((1, 1, 2 * tm), lambda i: (jnp.minimum(i + 1, n - 1), 0, 0),
                         memory_space=pltpu.SMEM),
            pl.BlockSpec(memory_space=pl.ANY),
            row(d), row(LANES), const(lng.shape), const(lnb.shape)],
        out_specs=[row(d), row(d)],
        out_shape=[jax.ShapeDtypeStruct((t, d), f32), jax.ShapeDtypeStruct((t, d), bf16)],
        scratch_shapes=[pltpu.VMEM((2, 2 * tm, d), f32), pltpu.SemaphoreType.DMA((2,))],
        compiler_params=pltpu.CompilerParams(
            dimension_semantics=("arbitrary",), vmem_limit_bytes=VMEM_LIMIT),
        name="combine_ln",
    )(pos, pos, ys, h1, gate, lng, lnb)


def _rope_tables(positions, dilation):
    b, s = positions.shape
    half = ROPE_DIM // 2
    inv_freq = ROPE_THETA ** (-2.0 * jnp.arange(half, dtype=f32) / ROPE_DIM)
    ang = positions.astype(f32)[:, :, None] * inv_freq
    cos, sin = jnp.cos(ang), jnp.sin(ang)
    pad = LANES - ROPE_DIM
    cos_f = jnp.concatenate([cos, cos, jnp.ones((b, s, pad), f32)], axis=-1)
    sin_f = jnp.concatenate([-sin, sin, jnp.zeros((b, s, pad), f32)], axis=-1)
    regroup = lambda a: a.reshape(b, s // dilation, dilation, LANES).transpose(0, 2, 1, 3)
    return regroup(cos_f), regroup(sin_f)


def _route_plan(route, cnt, t, n_tiles):
    ids = route[:, 0:2]
    ranks = route[:, 2:4]
    counts = cnt[0, ROUTER_LANE0:ROUTER_LANE0 + N_EXPERTS].astype(jnp.int32)
    padded = (counts + EXPERT_TILE - 1) // EXPERT_TILE * EXPERT_TILE
    ends = jnp.cumsum(padded)
    offs = ends - padded
    pos = offs[ids] + ranks
    n_used = (ends[-1] // EXPERT_TILE).astype(jnp.int32)
    tile_start = jnp.arange(n_tiles, dtype=jnp.int32) * EXPERT_TILE
    tile_expert = jnp.sum((tile_start[:, None] >= ends[None, :]).astype(jnp.int32), axis=1)
    tile_expert = jnp.minimum(tile_expert, N_EXPERTS - 1)
    last_used = tile_expert[jnp.maximum(n_used - 1, 0)]
    tile_expert = jnp.where(jnp.arange(n_tiles) < n_used, tile_expert, last_used)
    tok = jnp.repeat(jnp.arange(t, dtype=jnp.int32), 2)
    src = jnp.zeros((n_tiles * EXPERT_TILE,), jnp.int32).at[pos.reshape(-1)].set(tok)
    return pos, src.reshape(n_tiles, 1, EXPERT_TILE), tile_expert, n_used.reshape(1)


def kernel(x, positions, w_in, p_a, p_b, w_o, ln_g, ln_b, router_coarse, router_coarse_bias,
           router_fine, router_fine_bias, w_up, w_down):
    b, s, d = x.shape
    depth = w_in.shape[0]
    t = b * s
    alpha = float((2 * depth) ** 0.25)
    assert s % (DIL_CONFIGS[-1][1] * BLK) == 0 and s % SB_Q == 0 and d % GROUP_W == 0

    sizes = (("ga", d), ("gb", d), ("qb", WIDTH_B), ("kb", WIDTH_B), ("vb", WIDTH_B))
    col, off = {}, 0
    for name, width in sizes:
        col[name] = off
        off += width
    a3, b3 = 3 * WIDTH_A, 3 * WIDTH_B

    tables = [_rope_tables(positions, dilation) for _, dilation in DIL_CONFIGS]
    tm_proj = min(1024, s)
    tm_mix = min(256, t)
    tm_comb = min(256, t)
    n_tiles = (2 * t) // EXPERT_TILE + N_EXPERTS

    h = x.reshape(t, d)
    hb = h.astype(bf16)
    for l in range(depth):
        w = w_in[l]
        w_main = jnp.concatenate([w[:, a3 + b3:], w[:, a3:a3 + b3]], axis=1).astype(bf16)
        main = _in_proj(hb, w_main, b, s, 1, tm_proj, GROUP_W)
        main2 = main.reshape(t, main.shape[-1])
        outs, lses = [], []
        for g, (_, dilation) in enumerate(DIL_CONFIGS):
            w_g = jnp.concatenate(
                [w[:, c * WIDTH_A + g * GROUP_W:c * WIDTH_A + (g + 1) * GROUP_W] for c in range(3)],
                axis=1).astype(bf16)
            qkv = _in_proj(hb, w_g, b, s, dilation, tm_proj, GROUP_W)
            o, lse = _dilated_attention(qkv, tables[g][0], tables[g][1], dilation)
            outs.append(o)
            lses.append(lse)
        o_b = _stick_breaking(main.reshape(b, s, main.shape[-1]), col)

        wr = jnp.zeros((d, LANES), f32)
        wr = wr.at[:, :N_EXPERT_GROUPS].set(router_coarse[l])
        wr = wr.at[:, ROUTER_LANE0:ROUTER_LANE0 + N_EXPERTS].set(router_fine[l]).astype(bf16)
        br = jnp.zeros((1, LANES), f32)
        br = br.at[0, :N_EXPERT_GROUPS].set(router_coarse_bias[l])
        br = br.at[0, ROUTER_LANE0:ROUTER_LANE0 + N_EXPERTS].set(router_fine_bias[l])
        h1, route, gate, cnt = _mix(
            outs, lses, o_b, main2, h, p_a[l].astype(bf16), p_b[l].astype(bf16), w_o[l].astype(bf16),
            ln_g[l, 0].reshape(1, d), ln_b[l, 0].reshape(1, d), wr, br, col, alpha, tm_mix)

        pos, src, tile_expert, n_used = _route_plan(route, cnt, t, n_tiles)
        ys = _experts(h1, src, tile_expert, n_used, w_up[l].astype(bf16), w_down[l].astype(bf16))
        pos_tiles = pos.reshape(t // tm_comb, tm_comb, 2).transpose(0, 2, 1).reshape(t // tm_comb, 1, 2 * tm_comb)
        h, hb = _combine(pos_tiles, ys, h1, gate, ln_g[l, 1].reshape(1, d), ln_b[l, 1].reshape(1, d),
                         alpha, tm_comb)
    return h.reshape(b, s, d)
```

```python
import functools

import jax
import jax.numpy as jnp
from jax import lax
from jax.experimental import pallas as pl
from jax.experimental.pallas import tpu as pltpu

HEAD_DIM = 128
DIL_CONFIGS = ((128, 1), (512, 4), (2048, 16))
N_DIL_GROUPS = 3
HEADS_PER_DIL_GROUP = 4
N_SB_HEADS = 8
ROPE_THETA = 500000.0
ROPE_DIM = HEAD_DIM // 4
N_EXPERT_GROUPS = 4
EXPERTS_PER_GROUP = 4
N_EXPERTS = N_EXPERT_GROUPS * EXPERTS_PER_GROUP
LN_EPS = 1e-5

WIDTH_A = N_DIL_GROUPS * HEADS_PER_DIL_GROUP * HEAD_DIM
GROUP_W = HEADS_PER_DIL_GROUP * HEAD_DIM
WIDTH_B = N_SB_HEADS * HEAD_DIM

BLK = 128
LANES = 128
DIL_RES_PER_STEP = 4
SB_Q = 256
SB_K = 256
SB_HEADS_PER_STEP = 4
LOG2_E = 1.4426950408889634
SB_DEAD_LOG2 = -151.0
ROUTER_LANE0 = N_EXPERT_GROUPS
EXPERT_TILE = 256
VMEM_LIMIT = 56 * 1024 * 1024
NEG_BIG = -1e30

f32 = jnp.float32
bf16 = jnp.bfloat16


def _in_proj_kernel(x_ref, w_ref, o_ref, *scratch, dilation):
    acc = jnp.dot(x_ref[...], w_ref[0].astype(bf16), preferred_element_type=f32)
    if dilation == 1:
        o_ref[0, 0] = acc.astype(o_ref.dtype)
        return
    (acc_ref,) = scratch
    n = acc.shape[0] // dilation
    for s in range(acc.shape[1] // LANES):
        acc_ref[s] = acc[:, s * LANES:(s + 1) * LANES]
    for r in range(dilation):
        for s in range(acc.shape[1] // LANES):
            o_ref[0, r, :, s * LANES:(s + 1) * LANES] = (
                acc_ref[s, pl.ds(r, n, stride=dilation), :].astype(o_ref.dtype))


def _in_proj(x, w_in, layer, col_block, n_out, b, s, dilation, tm, tn):
    k = x.shape[1]
    per_b = s // tm
    scratch = [] if dilation == 1 else [pltpu.VMEM((tn // LANES, tm, LANES), f32)]
    return pl.pallas_call(
        functools.partial(_in_proj_kernel, dilation=dilation),
        grid=(b * per_b, n_out // tn),
        in_specs=[pl.BlockSpec((tm, k), lambda i, j: (i, 0)),
                  pl.BlockSpec((1, k, tn), lambda i, j: (layer, 0, col_block(j)))],
        out_specs=pl.BlockSpec((1, dilation, tm // dilation, tn),
                               lambda i, j: (i // per_b, 0, i % per_b, j)),
        out_shape=jax.ShapeDtypeStruct((b, dilation, s // dilation, n_out), bf16),
        scratch_shapes=scratch,
        compiler_params=pltpu.CompilerParams(
            dimension_semantics=("parallel", "parallel"), vmem_limit_bytes=VMEM_LIMIT),
        name=f"in_proj_d{dilation}",
    )(x, w_in)


def _rotate(t, cos_f, sin_f, lane):
    swapped = jnp.where(lane < ROPE_DIM // 2,
                        pltpu.roll(t, LANES - ROPE_DIM // 2, 1),
                        pltpu.roll(t, ROPE_DIM // 2, 1))
    return t * cos_f + swapped * sin_f


def _dil_kernel(q_ref, kc_ref, kp_ref, vc_ref, vp_ref, cc_ref, sc_ref, cp_ref, sp_ref,
                o_ref, lse_ref, o_scr, lse_scr, *, dilation, res_per_step):
    n = pl.program_id(1)
    rr = pl.program_id(2)
    lane = lax.broadcasted_iota(jnp.int32, (BLK, LANES), 1)
    qi = lax.broadcasted_iota(jnp.int32, (BLK, 2 * BLK), 0)
    kj = lax.broadcasted_iota(jnp.int32, (BLK, 2 * BLK), 1)
    first_key = jnp.where(n > 0, 0, BLK)
    valid = (kj >= jnp.maximum(qi, first_key)) & (kj <= qi + BLK)
    scale = HEAD_DIM ** -0.5

    def residue(ri, _):
        cos_c, sin_c, cos_p, sin_p = cc_ref[0, ri], sc_ref[0, ri], cp_ref[0, ri], sp_ref[0, ri]
        rows = pl.ds(rr * res_per_step + ri, BLK, stride=dilation) if dilation > 1 else slice(None)
        for h in range(HEADS_PER_DIL_GROUP):
            sl = slice(h * HEAD_DIM, (h + 1) * HEAD_DIM)
            q = _rotate(q_ref[0, ri, :, sl].astype(f32), cos_c, sin_c, lane).astype(bf16)
            kc = _rotate(kc_ref[0, ri, :, sl].astype(f32), cos_c, sin_c, lane).astype(bf16)
            kp = _rotate(kp_ref[0, ri, :, sl].astype(f32), cos_p, sin_p, lane).astype(bf16)
            k = jnp.concatenate([kp, kc], axis=0)
            v = jnp.concatenate([vp_ref[0, ri, :, sl], vc_ref[0, ri, :, sl]], axis=0)
            s = lax.dot_general(q, k, (((1,), (1,)), ((), ())), preferred_element_type=f32) * scale
            s = jnp.where(valid, s, NEG_BIG)
            m = jnp.max(s, axis=-1, keepdims=True)
            p = jnp.exp(s - m)
            den = jnp.sum(p, axis=-1, keepdims=True)
            o_scr[h, rows, :] = jnp.dot(p.astype(bf16), v, preferred_element_type=f32) / den
            lse_scr[h, rows, :] = jnp.broadcast_to(m + jnp.log(den), (BLK, HEAD_DIM))
        return 0

    lax.fori_loop(0, res_per_step, residue, 0)

    @pl.when(rr == pl.num_programs(2) - 1)
    def _():
        for h in range(HEADS_PER_DIL_GROUP):
            sl = slice(h * HEAD_DIM, (h + 1) * HEAD_DIM)
            o_ref[0, :, sl] = o_scr[h].astype(o_ref.dtype)
            lse_ref[0, :, sl] = lse_scr[h]


def _dilated_attention(qkv, cos_r, sin_r, dilation):
    b, _, sub, _ = qkv.shape
    s = sub * dilation
    nb = sub // BLK
    rs = min(dilation, DIL_RES_PER_STEP)

    def cur(c):
        return pl.BlockSpec((1, rs, BLK, GROUP_W), lambda bi, n, rr: (bi, rr, n, c))

    def prev(c):
        return pl.BlockSpec((1, rs, BLK, GROUP_W), lambda bi, n, rr: (bi, rr, jnp.maximum(n - 1, 0), c))

    tab_cur = pl.BlockSpec((1, rs, BLK, LANES), lambda bi, n, rr: (bi, rr, n, 0))
    tab_prev = pl.BlockSpec((1, rs, BLK, LANES), lambda bi, n, rr: (bi, rr, jnp.maximum(n - 1, 0), 0))
    out_spec = pl.BlockSpec((1, dilation * BLK, GROUP_W), lambda bi, n, rr: (bi, n, 0))
    o, lse = pl.pallas_call(
        functools.partial(_dil_kernel, dilation=dilation, res_per_step=rs),
        grid=(b, nb, dilation // rs),
        in_specs=[cur(0), cur(1), prev(1), cur(2), prev(2), tab_cur, tab_cur, tab_prev, tab_prev],
        out_specs=[out_spec, out_spec],
        out_shape=[jax.ShapeDtypeStruct((b, s, GROUP_W), bf16),
                   jax.ShapeDtypeStruct((b, s, GROUP_W), f32)],
        scratch_shapes=[pltpu.VMEM((HEADS_PER_DIL_GROUP, dilation * BLK, HEAD_DIM), f32),
                        pltpu.VMEM((HEADS_PER_DIL_GROUP, dilation * BLK, HEAD_DIM), f32)],
        compiler_params=pltpu.CompilerParams(
            dimension_semantics=("parallel", "arbitrary", "arbitrary"), vmem_limit_bytes=VMEM_LIMIT),
        name=f"dilated_attn_d{dilation}",
    )(qkv, qkv, qkv, qkv, qkv, cos_r, sin_r, cos_r, sin_r)
    return o.reshape(b * s, GROUP_W), lse.reshape(b * s, GROUP_W)


def _sb_kernel(q_ref, k_ref, v_ref, up_ref, o_ref, acc_ref, c_ref):
    i = pl.program_id(2)
    scale2 = HEAD_DIM ** -0.5 * LOG2_E
    row = lax.broadcasted_iota(jnp.int32, (SB_Q, SB_K), 0)
    coli = lax.broadcasted_iota(jnp.int32, (SB_Q, SB_K), 1)
    causal = coli < row
    acc_ref[...] = jnp.zeros_like(acc_ref)
    c_ref[...] = jnp.zeros_like(c_ref)

    def chunk(j, diag):
        start = pl.multiple_of(j * SB_K, SB_K)
        heads = range(SB_HEADS_PER_STEP)
        sls = [slice(hd * HEAD_DIM, (hd + 1) * HEAD_DIM) for hd in heads]
        ts = [lax.dot_general(q_ref[0, :, sl], k_ref[0, pl.ds(start, SB_K), sl],
                              (((1,), (1,)), ((), ())), preferred_element_type=f32) * scale2
              for sl in sls]
        lss, lks, splits = [], [], []
        for t in ts:
            ls = jnp.minimum(t, 0.0) - jnp.log2(1.0 + jnp.exp2(-jnp.abs(t)))
            lk = ls - t
            if diag:
                lk = jnp.where(causal, lk, 0.0)
            hi = lk.astype(bf16)
            lo = (lk - hi.astype(f32)).astype(bf16)
            lss.append(ls)
            lks.append(lk)
            splits.append(jnp.concatenate([hi, lo], axis=1))
        laters = [jnp.dot(x, up_ref[...], preferred_element_type=f32) for x in splits]
        probs = []
        for hd in heads:
            c = c_ref[hd]
            a = jnp.exp2(lss[hd] + laters[hd] + jnp.concatenate([c, c], axis=1))
            if diag:
                a = jnp.where(causal, a, 0.0)
            probs.append(a.astype(bf16))
            c_ref[hd] = c + jnp.sum(lks[hd], axis=1, keepdims=True)
        for hd in heads:
            acc_ref[hd] += jnp.dot(probs[hd], v_ref[0, pl.ds(start, SB_K), sls[hd]],
                                   preferred_element_type=f32)

    chunk(i, True)

    def live():
        return jnp.max(c_ref[...]) > SB_DEAD_LOG2

    def cond(state):
        t, go = state
        return jnp.logical_and(t < i, go)

    def body(state):
        t, _ = state
        chunk(i - 1 - t, False)
        return t + 1, live()

    lax.while_loop(cond, body, (jnp.int32(0), live()))
    for hd in range(SB_HEADS_PER_STEP):
        o_ref[0, :, hd * HEAD_DIM:(hd + 1) * HEAD_DIM] = acc_ref[hd].astype(o_ref.dtype)


def _stick_breaking(main3, col):
    b, s, _ = main3.shape
    w = SB_HEADS_PER_STEP * HEAD_DIM
    qb, kb, vb = col["qb"] // w, col["kb"] // w, col["vb"] // w
    jr = lax.broadcasted_iota(jnp.int32, (2 * SB_K, SB_K), 0)
    sc = lax.broadcasted_iota(jnp.int32, (2 * SB_K, SB_K), 1)
    upper2 = jnp.where((jr % SB_K) > sc, 1.0, 0.0).astype(bf16)
    o = pl.pallas_call(
        _sb_kernel,
        grid=(b, N_SB_HEADS // SB_HEADS_PER_STEP, s // SB_Q),
        in_specs=[pl.BlockSpec((1, SB_Q, w), lambda bi, h, i: (bi, i, qb + h)),
                  pl.BlockSpec((1, s, w), lambda bi, h, i: (bi, 0, kb + h)),
                  pl.BlockSpec((1, s, w), lambda bi, h, i: (bi, 0, vb + h)),
                  pl.BlockSpec((2 * SB_K, SB_K), lambda bi, h, i: (0, 0))],
        out_specs=pl.BlockSpec((1, SB_Q, w), lambda bi, h, i: (bi, i, h)),
        out_shape=jax.ShapeDtypeStruct((b, s, WIDTH_B), bf16),
        scratch_shapes=[pltpu.VMEM((SB_HEADS_PER_STEP, SB_Q, HEAD_DIM), f32),
                        pltpu.VMEM((SB_HEADS_PER_STEP, SB_Q, HEAD_DIM), f32)],
        compiler_params=pltpu.CompilerParams(
            dimension_semantics=("parallel", "parallel", "arbitrary"), vmem_limit_bytes=VMEM_LIMIT),
        name="stick_breaking_attn",
    )(main3, main3, main3, upper2)
    return o.reshape(b * s, WIDTH_B)


def _layer_norm(u, g, b):
    mu = jnp.mean(u, axis=-1, keepdims=True)
    d = u - mu
    var = jnp.mean(d * d, axis=-1, keepdims=True)
    return d * lax.rsqrt(var + LN_EPS) * g + b


def _mix_kernel(o0_ref, o1_ref, o2_ref, l0_ref, l1_ref, l2_ref, ob_ref, ga_ref, gb_ref, h_ref,
                pa_ref, pb_ref, wo_ref, lng_ref, lnb_ref, wr_ref, br_ref,
                h1_ref, route_ref, gate_ref, cnt_ref, carry_ref, *, alpha):
    step = pl.program_id(0)

    @pl.when(step == 0)
    def _():
        carry_ref[...] = jnp.zeros_like(carry_ref)

    l0, l1, l2 = l0_ref[...], l1_ref[...], l2_ref[...]
    m = jnp.maximum(jnp.maximum(l0, l1), l2)
    e0, e1, e2 = jnp.exp(l0 - m), jnp.exp(l1 - m), jnp.exp(l2 - m)
    o_a = (e0 * o0_ref[...].astype(f32) + e1 * o1_ref[...].astype(f32)
           + e2 * o2_ref[...].astype(f32)) / (e0 + e1 + e2)
    ya = jnp.dot(o_a.astype(bf16), pa_ref[...], preferred_element_type=f32)
    yb = jnp.dot(ob_ref[...], pb_ref[...], preferred_element_type=f32)
    merged = (jax.nn.sigmoid(ga_ref[...].astype(f32)) * ya
              + jax.nn.sigmoid(gb_ref[...].astype(f32)) * yb)
    y = jnp.dot(merged.astype(bf16), wo_ref[...], preferred_element_type=f32)
    h1 = _layer_norm(alpha * h_ref[...] + y, lng_ref[...], lnb_ref[...])
    h1_ref[...] = h1
    h1b = h1.astype(bf16)

    tm = h1.shape[0]
    logits = jnp.dot(h1b, wr_ref[...], preferred_element_type=f32) + br_ref[...]
    lane = lax.broadcasted_iota(jnp.int32, (tm, LANES), 1).astype(f32)
    no_lane = float(LANES)
    lc = jnp.where(lane < N_EXPERT_GROUPS, logits, NEG_BIG)
    mc = jnp.max(lc, axis=-1, keepdims=True)
    p_grp = 1.0 / jnp.sum(jnp.exp(lc - mc), axis=-1, keepdims=True)
    g_idx = jnp.min(jnp.where(lc == mc, lane, no_lane), axis=-1, keepdims=True)
    lo_lane = ROUTER_LANE0 + EXPERTS_PER_GROUP * g_idx
    sel = (lane >= lo_lane) & (lane < lo_lane + EXPERTS_PER_GROUP)
    lf = jnp.where(sel, logits, NEG_BIG)
    v1 = jnp.max(lf, axis=-1, keepdims=True)
    i1 = jnp.min(jnp.where(lf == v1, lane, no_lane), axis=-1, keepdims=True)
    lf2 = jnp.where(lane == i1, NEG_BIG, lf)
    v2 = jnp.max(lf2, axis=-1, keepdims=True)
    i2 = jnp.min(jnp.where((lf2 == v2) & sel & (lane != i1), lane, no_lane), axis=-1, keepdims=True)
    t = jnp.exp(v2 - v1)
    w1 = 1.0 / (1.0 + t)
    w2 = t / (1.0 + t)
    gate_ref[...] = jnp.where(lane == 0, p_grp * w1, jnp.where(lane == 1, p_grp * w2, 0.0))

    hit1 = lane == i1
    hit2 = lane == i2
    onehot = jnp.where(hit1 | hit2, 1.0, 0.0)
    r = lax.broadcasted_iota(jnp.int32, (tm, tm), 0)
    c = lax.broadcasted_iota(jnp.int32, (tm, tm), 1)
    lower = jnp.where(c < r, 1.0, 0.0).astype(bf16)
    before = jnp.dot(lower, onehot.astype(bf16), preferred_element_type=f32) + carry_ref[0:1, :]
    r1 = jnp.sum(jnp.where(hit1, before, 0.0), axis=-1, keepdims=True)
    r2 = jnp.sum(jnp.where(hit2, before, 0.0), axis=-1, keepdims=True)
    route = jnp.where(lane == 0, i1 - ROUTER_LANE0,
                      jnp.where(lane == 1, i2 - ROUTER_LANE0,
                                jnp.where(lane == 2, r1, jnp.where(lane == 3, r2, 0.0))))
    route_ref[...] = route.astype(jnp.int32)
    new_carry = carry_ref[0:1, :] + jnp.sum(onehot, axis=0, keepdims=True)
    carry_ref[...] = jnp.broadcast_to(new_carry, carry_ref.shape)
    cnt_ref[...] = jnp.broadcast_to(new_carry, cnt_ref.shape)


def _mix(outs, lses, o_b, main2, h, pa, pb, wo, lng, lnb, wr, br, col, alpha, tm):
    t, d = h.shape
    row = lambda width, c: pl.BlockSpec((tm, width), lambda i: (i, c))
    const = lambda shape: pl.BlockSpec(shape, lambda i: (0, 0), pipeline_mode=pl.Buffered(1))
    in_specs = ([row(GROUP_W, 0)] * 6
                + [row(WIDTH_B, 0), row(d, col["ga"] // d), row(d, col["gb"] // d), row(d, 0),
                   const(pa.shape), const(pb.shape), const(wo.shape),
                   const(lng.shape), const(lnb.shape), const(wr.shape), const(br.shape)])
    return pl.pallas_call(
        functools.partial(_mix_kernel, alpha=alpha),
        grid=(t // tm,),
        in_specs=in_specs,
        out_specs=[row(d, 0), row(LANES, 0), row(LANES, 0),
                   pl.BlockSpec((8, LANES), lambda i: (0, 0))],
        out_shape=[jax.ShapeDtypeStruct((t, d), f32),
                   jax.ShapeDtypeStruct((t, LANES), jnp.int32), jax.ShapeDtypeStruct((t, LANES), f32),
                   jax.ShapeDtypeStruct((8, LANES), f32)],
        scratch_shapes=[pltpu.VMEM((8, LANES), f32)],
        compiler_params=pltpu.CompilerParams(
            dimension_semantics=("arbitrary",), vmem_limit_bytes=VMEM_LIMIT),
        name="mix_ln_router",
    )(*outs, *lses, o_b, main2, main2, h, pa, pb, wo, lng, lnb, wr, br)


def _gather_rows(idx_ref, n_rows, src_hbm, dst, sem):
    def body(r, _):
        pltpu.make_async_copy(src_hbm.at[pl.ds(idx_ref[0, 0, r], 1)], dst.at[pl.ds(r, 1)], sem).start()
        return 0
    lax.fori_loop(0, n_rows, body, 0, unroll=8)


def _expert_kernel(te_ref, nu_ref, src_cur, src_nxt, x_hbm, wu_ref, wd_ref, o_ref, xbuf, sem):
    i = pl.program_id(0)
    n = pl.num_programs(0)
    slot = i % 2
    ff = wd_ref.shape[1]

    @pl.when(i == 0)
    def _():
        _gather_rows(src_cur, EXPERT_TILE, x_hbm, xbuf.at[0], sem.at[0])

    @pl.when(i + 1 < n)
    def _():
        _gather_rows(src_nxt, EXPERT_TILE, x_hbm, xbuf.at[1 - slot], sem.at[1 - slot])

    pltpu.make_async_copy(x_hbm.at[pl.ds(0, EXPERT_TILE)], xbuf.at[slot], sem.at[slot]).wait()

    @pl.when(i < nu_ref[0])
    def _():
        x = xbuf[slot].astype(bf16)
        hh = jnp.dot(x, wu_ref[0], preferred_element_type=f32)
        a, b = hh[:, :ff], hh[:, ff:]
        act = a * jax.nn.sigmoid(a) * b
        o_ref[...] = jnp.dot(act.astype(bf16), wd_ref[0], preferred_element_type=f32)

    @pl.when(i >= nu_ref[0])
    def _():
        o_ref[...] = jnp.zeros_like(o_ref)


def _experts(h1, src, tile_expert, n_used, wu, wd):
    t, d = h1.shape
    n_tiles = src.shape[0]
    ff2 = wu.shape[2]
    ff = wd.shape[1]
    grid_spec = pltpu.PrefetchScalarGridSpec(
        num_scalar_prefetch=2,
        grid=(n_tiles,),
        in_specs=[
            pl.BlockSpec((1, 1, EXPERT_TILE), lambda i, te, nu: (i, 0, 0), memory_space=pltpu.SMEM),
            pl.BlockSpec((1, 1, EXPERT_TILE), lambda i, te, nu: (jnp.minimum(i + 1, n_tiles - 1), 0, 0),
                         memory_space=pltpu.SMEM),
            pl.BlockSpec(memory_space=pl.ANY),
            pl.BlockSpec((1, d, ff2), lambda i, te, nu: (te[i], 0, 0)),
            pl.BlockSpec((1, ff, d), lambda i, te, nu: (te[i], 0, 0)),
        ],
        out_specs=pl.BlockSpec((EXPERT_TILE, d), lambda i, te, nu: (i, 0)),
        scratch_shapes=[pltpu.VMEM((2, EXPERT_TILE, d), f32), pltpu.SemaphoreType.DMA((2,))],
    )
    return pl.pallas_call(
        _expert_kernel,
        grid_spec=grid_spec,
        out_shape=jax.ShapeDtypeStruct((n_tiles * EXPERT_TILE, d), f32),
        compiler_params=pltpu.CompilerParams(
            dimension_semantics=("arbitrary",), vmem_limit_bytes=VMEM_LIMIT),
        name="experts",
    )(tile_expert, n_used, src, src, h1, wu, wd)


def _combine_kernel(pos_cur, pos_nxt, ys_hbm, h1_ref, gate_ref, lng_ref, lnb_ref,
                    h2_ref, h2b_ref, ybuf, sem, *, alpha, tm):
    i = pl.program_id(0)
    n = pl.num_programs(0)
    slot = i % 2

    @pl.when(i == 0)
    def _():
        _gather_rows(pos_cur, 2 * tm, ys_hbm, ybuf.at[0], sem.at[0])

    @pl.when(i + 1 < n)
    def _():
        _gather_rows(pos_nxt, 2 * tm, ys_hbm, ybuf.at[1 - slot], sem.at[1 - slot])

    pltpu.make_async_copy(ys_hbm.at[pl.ds(0, 2 * tm)], ybuf.at[slot], sem.at[slot]).wait()
    g = gate_ref[...]
    y = g[:, 0:1] * ybuf[slot, 0:tm, :] + g[:, 1:2] * ybuf[slot, tm:2 * tm, :]
    h2 = _layer_norm(alpha * h1_ref[...] + y, lng_ref[...], lnb_ref[...])
    h2_ref[...] = h2
    h2b_ref[...] = h2.astype(bf16)


def _combine(pos, ys, h1, gate, lng, lnb, alpha, tm):
    t, d = h1.shape
    n = t // tm
    row = lambda width: pl.BlockSpec((tm, width), lambda i: (i, 0))
    const = lambda shape: pl.BlockSpec(shape, lambda i: (0, 0))
    return pl.pallas_call(
        functools.partial(_combine_kernel, alpha=alpha, tm=tm),
        grid=(n,),
        in_specs=[
            pl.BlockSpec((1, 1, 2 * tm), lambda i: (i, 0, 0), memory_space=pltpu.SMEM),
            pl.BlockSpec((1, 1, 2 * tm), lambda i: (jnp.minimum(i + 1, n - 1), 0, 0),
                         memory_space=pltpu.SMEM),
            pl.BlockSpec(memory_space=pl.ANY),
            row(d), row(LANES), const(lng.shape), const(lnb.shape)],
        out_specs=[row(d), row(d)],
        out_shape=[jax.ShapeDtypeStruct((t, d), f32), jax.ShapeDtypeStruct((t, d), bf16)],
        scratch_shapes=[pltpu.VMEM((2, 2 * tm, d), f32), pltpu.SemaphoreType.DMA((2,))],
        compiler_params=pltpu.CompilerParams(
            dimension_semantics=("arbitrary",), vmem_limit_bytes=VMEM_LIMIT),
        name="combine_ln",
    )(pos, pos, ys, h1, gate, lng, lnb)


def _rope_tables(positions, dilation):
    b, s = positions.shape
    half = ROPE_DIM // 2
    inv_freq = ROPE_THETA ** (-2.0 * jnp.arange(half, dtype=f32) / ROPE_DIM)
    ang = positions.astype(f32)[:, :, None] * inv_freq
    cos, sin = jnp.cos(ang), jnp.sin(ang)
    pad = LANES - ROPE_DIM
    cos_f = jnp.concatenate([cos, cos, jnp.ones((b, s, pad), f32)], axis=-1)
    sin_f = jnp.concatenate([-sin, sin, jnp.zeros((b, s, pad), f32)], axis=-1)
    regroup = lambda a: a.reshape(b, s // dilation, dilation, LANES).transpose(0, 2, 1, 3)
    return regroup(cos_f), regroup(sin_f)


def _route_plan(route, cnt, t, n_tiles):
    ids = route[:, 0:2]
    ranks = route[:, 2:4]
    counts = cnt[0, ROUTER_LANE0:ROUTER_LANE0 + N_EXPERTS].astype(jnp.int32)
    padded = (counts + EXPERT_TILE - 1) // EXPERT_TILE * EXPERT_TILE
    ends = jnp.cumsum(padded)
    offs = ends - padded
    pos = offs[ids] + ranks
    n_used = (ends[-1] // EXPERT_TILE).astype(jnp.int32)
    tile_start = jnp.arange(n_tiles, dtype=jnp.int32) * EXPERT_TILE
    tile_expert = jnp.sum((tile_start[:, None] >= ends[None, :]).astype(jnp.int32), axis=1)
    tile_expert = jnp.minimum(tile_expert, N_EXPERTS - 1)
    last_used = tile_expert[jnp.maximum(n_used - 1, 0)]
    tile_expert = jnp.where(jnp.arange(n_tiles) < n_used, tile_expert, last_used)
    tok = jnp.repeat(jnp.arange(t, dtype=jnp.int32), 2)
    src = jnp.zeros((n_tiles * EXPERT_TILE,), jnp.int32).at[pos.reshape(-1)].set(tok)
    return pos, src.reshape(n_tiles, 1, EXPERT_TILE), tile_expert, n_used.reshape(1)


def kernel(x, positions, w_in, p_a, p_b, w_o, ln_g, ln_b, router_coarse, router_coarse_bias,
           router_fine, router_fine_bias, w_up, w_down):
    b, s, d = x.shape
    depth = w_in.shape[0]
    t = b * s
    alpha = float((2 * depth) ** 0.25)
    assert s % (DIL_CONFIGS[-1][1] * BLK) == 0 and s % SB_Q == 0 and d % GROUP_W == 0

    sizes = (("ga", d), ("gb", d), ("qb", WIDTH_B), ("kb", WIDTH_B), ("vb", WIDTH_B))
    col, off = {}, 0
    for name, width in sizes:
        col[name] = off
        off += width
    a3, b3 = 3 * WIDTH_A, 3 * WIDTH_B

    tables = [_rope_tables(positions, dilation) for _, dilation in DIL_CONFIGS]
    tm_proj = min(2048, s)
    tm_mix = min(256, t)
    tm_comb = min(256, t)
    n_tiles = (2 * t) // EXPERT_TILE + N_EXPERTS

    h = x.reshape(t, d)
    hb = h.astype(bf16)
    for l in range(depth):
        gate_blocks = 2 * d // GROUP_W
        main_block = lambda j: jnp.where(j < gate_blocks, (a3 + b3) // GROUP_W + j,
                                         a3 // GROUP_W + j - gate_blocks)
        main = _in_proj(hb, w_in, l, main_block, 2 * d + b3, b, s, 1, tm_proj, GROUP_W)
        main2 = main.reshape(t, main.shape[-1])
        outs, lses = [], []
        for g, (_, dilation) in enumerate(DIL_CONFIGS):
            group_block = lambda j, g=g: j * N_DIL_GROUPS + g
            qkv = _in_proj(hb, w_in, l, group_block, 3 * GROUP_W, b, s, dilation, tm_proj, GROUP_W)
            o, lse = _dilated_attention(qkv, tables[g][0], tables[g][1], dilation)
            outs.append(o)
            lses.append(lse)
        o_b = _stick_breaking(main.reshape(b, s, main.shape[-1]), col)

        wr = jnp.zeros((d, LANES), f32)
        wr = wr.at[:, :N_EXPERT_GROUPS].set(router_coarse[l])
        wr = wr.at[:, ROUTER_LANE0:ROUTER_LANE0 + N_EXPERTS].set(router_fine[l]).astype(bf16)
        br = jnp.zeros((1, LANES), f32)
        br = br.at[0, :N_EXPERT_GROUPS].set(router_coarse_bias[l])
        br = br.at[0, ROUTER_LANE0:ROUTER_LANE0 + N_EXPERTS].set(router_fine_bias[l])
        h1, route, gate, cnt = _mix(
            outs, lses, o_b, main2, h, p_a[l].astype(bf16), p_b[l].astype(bf16), w_o[l].astype(bf16),
            ln_g[l, 0].reshape(1, d), ln_b[l, 0].reshape(1, d), wr, br, col, alpha, tm_mix)

        pos, src, tile_expert, n_used = _route_plan(route, cnt, t, n_tiles)
        ys = _experts(h1, src, tile_expert, n_used, w_up[l].astype(bf16), w_down[l].astype(bf16))
        pos_tiles = pos.reshape(t // tm_comb, tm_comb, 2).transpose(0, 2, 1).reshape(t // tm_comb, 1, 2 * tm_comb)
        h, hb = _combine(pos_tiles, ys, h1, gate, ln_g[l, 1].reshape(1, d), ln_b[l, 1].reshape(1, d),
                         alpha, tm_comb)
    return h.reshape(b, s, d)
```

```python
import functools

import jax
import jax.numpy as jnp
from jax import lax
from jax.experimental import pallas as pl
from jax.experimental.pallas import tpu as pltpu

HEAD_DIM = 128
DIL_CONFIGS = ((128, 1), (512, 4), (2048, 16))
N_DIL_GROUPS = 3
HEADS_PER_DIL_GROUP = 4
N_SB_HEADS = 8
ROPE_THETA = 500000.0
ROPE_DIM = HEAD_DIM // 4
N_EXPERT_GROUPS = 4
EXPERTS_PER_GROUP = 4
N_EXPERTS = N_EXPERT_GROUPS * EXPERTS_PER_GROUP
PAIRS_PER_GROUP = EXPERTS_PER_GROUP * (EXPERTS_PER_GROUP - 1) // 2
N_BUCKETS = N_EXPERT_GROUPS * PAIRS_PER_GROUP
LN_EPS = 1e-5

WIDTH_A = N_DIL_GROUPS * HEADS_PER_DIL_GROUP * HEAD_DIM
GROUP_W = HEADS_PER_DIL_GROUP * HEAD_DIM
WIDTH_B = N_SB_HEADS * HEAD_DIM

BLK = 128
LANES = 128
DIL_RES_PER_STEP = 4
SB_Q = 256
SB_K = 256
SB_HEADS_PER_STEP = 4
LOG2_E = 1.4426950408889634
SB_DEAD_LOG2 = -151.0
ROUTER_LANE0 = N_EXPERT_GROUPS
EXPERT_TILE = 256
VMEM_LIMIT = 56 * 1024 * 1024
NEG_BIG = -1e30

f32 = jnp.float32
bf16 = jnp.bfloat16


def _in_proj_kernel(x_ref, w_ref, o_ref, *scratch, dilation):
    acc = jnp.dot(x_ref[...], w_ref[0].astype(bf16), preferred_element_type=f32)
    if dilation == 1:
        o_ref[0, 0] = acc.astype(o_ref.dtype)
        return
    (acc_ref,) = scratch
    n = acc.shape[0] // dilation
    for s in range(acc.shape[1] // LANES):
        acc_ref[s] = acc[:, s * LANES:(s + 1) * LANES]
    for r in range(dilation):
        for s in range(acc.shape[1] // LANES):
            o_ref[0, r, :, s * LANES:(s + 1) * LANES] = (
                acc_ref[s, pl.ds(r, n, stride=dilation), :].astype(o_ref.dtype))


def _in_proj(x, w_in, layer, col_block, n_out, b, s, dilation, tm, tn):
    k = x.shape[1]
    per_b = s // tm
    scratch = [] if dilation == 1 else [pltpu.VMEM((tn // LANES, tm, LANES), f32)]
    return pl.pallas_call(
        functools.partial(_in_proj_kernel, dilation=dilation),
        grid=(b * per_b, n_out // tn),
        in_specs=[pl.BlockSpec((tm, k), lambda i, j: (i, 0)),
                  pl.BlockSpec((1, k, tn), lambda i, j: (layer, 0, col_block(j)))],
        out_specs=pl.BlockSpec((1, dilation, tm // dilation, tn),
                               lambda i, j: (i // per_b, 0, i % per_b, j)),
        out_shape=jax.ShapeDtypeStruct((b, dilation, s // dilation, n_out), bf16),
        scratch_shapes=scratch,
        compiler_params=pltpu.CompilerParams(
            dimension_semantics=("parallel", "parallel"), vmem_limit_bytes=VMEM_LIMIT),
        name=f"in_proj_d{dilation}",
    )(x, w_in)


def _rotate(t, cos_f, sin_f, lane):
    swapped = jnp.where(lane < ROPE_DIM // 2,
                        pltpu.roll(t, LANES - ROPE_DIM // 2, 1),
                        pltpu.roll(t, ROPE_DIM // 2, 1))
    return t * cos_f + swapped * sin_f


def _dil_kernel(q_ref, kc_ref, kp_ref, vc_ref, vp_ref, cc_ref, sc_ref, cp_ref, sp_ref,
                o_ref, lse_ref, o_scr, lse_scr, *, dilation, res_per_step):
    n = pl.program_id(1)
    rr = pl.program_id(2)
    lane = lax.broadcasted_iota(jnp.int32, (BLK, LANES), 1)
    qi = lax.broadcasted_iota(jnp.int32, (BLK, 2 * BLK), 0)
    kj = lax.broadcasted_iota(jnp.int32, (BLK, 2 * BLK), 1)
    first_key = jnp.where(n > 0, 0, BLK)
    valid = (kj >= jnp.maximum(qi, first_key)) & (kj <= qi + BLK)
    scale = HEAD_DIM ** -0.5

    def residue(ri, _):
        cos_c, sin_c, cos_p, sin_p = cc_ref[0, ri], sc_ref[0, ri], cp_ref[0, ri], sp_ref[0, ri]
        rows = pl.ds(rr * res_per_step + ri, BLK, stride=dilation) if dilation > 1 else slice(None)
        for h in range(HEADS_PER_DIL_GROUP):
            sl = slice(h * HEAD_DIM, (h + 1) * HEAD_DIM)
            q = _rotate(q_ref[0, ri, :, sl].astype(f32), cos_c, sin_c, lane).astype(bf16)
            kc = _rotate(kc_ref[0, ri, :, sl].astype(f32), cos_c, sin_c, lane).astype(bf16)
            kp = _rotate(kp_ref[0, ri, :, sl].astype(f32), cos_p, sin_p, lane).astype(bf16)
            k = jnp.concatenate([kp, kc], axis=0)
            v = jnp.concatenate([vp_ref[0, ri, :, sl], vc_ref[0, ri, :, sl]], axis=0)
            s = lax.dot_general(q, k, (((1,), (1,)), ((), ())), preferred_element_type=f32) * scale
            s = jnp.where(valid, s, NEG_BIG)
            m = jnp.max(s, axis=-1, keepdims=True)
            p = jnp.exp(s - m)
            den = jnp.sum(p, axis=-1, keepdims=True)
            o_scr[h, rows, :] = jnp.dot(p.astype(bf16), v, preferred_element_type=f32) / den
            lse_scr[h, rows, :] = jnp.broadcast_to(m + jnp.log(den), (BLK, HEAD_DIM))
        return 0

    lax.fori_loop(0, res_per_step, residue, 0)

    @pl.when(rr == pl.num_programs(2) - 1)
    def _():
        for h in range(HEADS_PER_DIL_GROUP):
            sl = slice(h * HEAD_DIM, (h + 1) * HEAD_DIM)
            o_ref[0, :, sl] = o_scr[h].astype(o_ref.dtype)
            lse_ref[0, :, sl] = lse_scr[h]


def _dilated_attention(qkv, cos_r, sin_r, dilation):
    b, _, sub, _ = qkv.shape
    s = sub * dilation
    nb = sub // BLK
    rs = min(dilation, DIL_RES_PER_STEP)

    def cur(c):
        return pl.BlockSpec((1, rs, BLK, GROUP_W), lambda bi, n, rr: (bi, rr, n, c))

    def prev(c):
        return pl.BlockSpec((1, rs, BLK, GROUP_W), lambda bi, n, rr: (bi, rr, jnp.maximum(n - 1, 0), c))

    tab_cur = pl.BlockSpec((1, rs, BLK, LANES), lambda bi, n, rr: (bi, rr, n, 0))
    tab_prev = pl.BlockSpec((1, rs, BLK, LANES), lambda bi, n, rr: (bi, rr, jnp.maximum(n - 1, 0), 0))
    out_spec = pl.BlockSpec((1, dilation * BLK, GROUP_W), lambda bi, n, rr: (bi, n, 0))
    o, lse = pl.pallas_call(
        functools.partial(_dil_kernel, dilation=dilation, res_per_step=rs),
        grid=(b, nb, dilation // rs),
        in_specs=[cur(0), cur(1), prev(1), cur(2), prev(2), tab_cur, tab_cur, tab_prev, tab_prev],
        out_specs=[out_spec, out_spec],
        out_shape=[jax.ShapeDtypeStruct((b, s, GROUP_W), bf16),
                   jax.ShapeDtypeStruct((b, s, GROUP_W), f32)],
        scratch_shapes=[pltpu.VMEM((HEADS_PER_DIL_GROUP, dilation * BLK, HEAD_DIM), f32),
                        pltpu.VMEM((HEADS_PER_DIL_GROUP, dilation * BLK, HEAD_DIM), f32)],
        compiler_params=pltpu.CompilerParams(
            dimension_semantics=("parallel", "arbitrary", "arbitrary"), vmem_limit_bytes=VMEM_LIMIT),
        name=f"dilated_attn_d{dilation}",
    )(qkv, qkv, qkv, qkv, qkv, cos_r, sin_r, cos_r, sin_r)
    return o.reshape(b * s, GROUP_W), lse.reshape(b * s, GROUP_W)


def _sb_kernel(q_ref, k_ref, v_ref, up_ref, o_ref, acc_ref, c_ref):
    i = pl.program_id(2)
    scale2 = HEAD_DIM ** -0.5 * LOG2_E
    row = lax.broadcasted_iota(jnp.int32, (SB_Q, SB_K), 0)
    coli = lax.broadcasted_iota(jnp.int32, (SB_Q, SB_K), 1)
    causal = coli < row
    acc_ref[...] = jnp.zeros_like(acc_ref)
    c_ref[...] = jnp.zeros_like(c_ref)

    def chunk(j, diag):
        start = pl.multiple_of(j * SB_K, SB_K)
        heads = range(SB_HEADS_PER_STEP)
        sls = [slice(hd * HEAD_DIM, (hd + 1) * HEAD_DIM) for hd in heads]
        ts = [lax.dot_general(q_ref[0, :, sl], k_ref[0, pl.ds(start, SB_K), sl],
                              (((1,), (1,)), ((), ())), preferred_element_type=f32) * scale2
              for sl in sls]
        lss, lks, splits = [], [], []
        for t in ts:
            ls = jnp.minimum(t, 0.0) - jnp.log2(1.0 + jnp.exp2(-jnp.abs(t)))
            lk = ls - t
            if diag:
                lk = jnp.where(causal, lk, 0.0)
            hi = lk.astype(bf16)
            lo = (lk - hi.astype(f32)).astype(bf16)
            lss.append(ls)
            lks.append(lk)
            splits.append(jnp.concatenate([hi, lo], axis=1))
        laters = [jnp.dot(x, up_ref[...], preferred_element_type=f32) for x in splits]
        probs = []
        for hd in heads:
            c = c_ref[hd]
            a = jnp.exp2(lss[hd] + laters[hd] + jnp.concatenate([c, c], axis=1))
            if diag:
                a = jnp.where(causal, a, 0.0)
            probs.append(a.astype(bf16))
            c_ref[hd] = c + jnp.sum(lks[hd], axis=1, keepdims=True)
        for hd in heads:
            acc_ref[hd] += jnp.dot(probs[hd], v_ref[0, pl.ds(start, SB_K), sls[hd]],
                                   preferred_element_type=f32)

    chunk(i, True)

    def live():
        return jnp.max(c_ref[...]) > SB_DEAD_LOG2

    def cond(state):
        t, go = state
        return jnp.logical_and(t < i, go)

    def body(state):
        t, _ = state
        chunk(i - 1 - t, False)
        return t + 1, live()

    lax.while_loop(cond, body, (jnp.int32(0), live()))
    for hd in range(SB_HEADS_PER_STEP):
        o_ref[0, :, hd * HEAD_DIM:(hd + 1) * HEAD_DIM] = acc_ref[hd].astype(o_ref.dtype)


def _stick_breaking(main3, col):
    b, s, _ = main3.shape
    w = SB_HEADS_PER_STEP * HEAD_DIM
    qb, kb, vb = col["qb"] // w, col["kb"] // w, col["vb"] // w
    jr = lax.broadcasted_iota(jnp.int32, (2 * SB_K, SB_K), 0)
    sc = lax.broadcasted_iota(jnp.int32, (2 * SB_K, SB_K), 1)
    upper2 = jnp.where((jr % SB_K) > sc, 1.0, 0.0).astype(bf16)
    o = pl.pallas_call(
        _sb_kernel,
        grid=(b, N_SB_HEADS // SB_HEADS_PER_STEP, s // SB_Q),
        in_specs=[pl.BlockSpec((1, SB_Q, w), lambda bi, h, i: (bi, i, qb + h)),
                  pl.BlockSpec((1, s, w), lambda bi, h, i: (bi, 0, kb + h)),
                  pl.BlockSpec((1, s, w), lambda bi, h, i: (bi, 0, vb + h)),
                  pl.BlockSpec((2 * SB_K, SB_K), lambda bi, h, i: (0, 0))],
        out_specs=pl.BlockSpec((1, SB_Q, w), lambda bi, h, i: (bi, i, h)),
        out_shape=jax.ShapeDtypeStruct((b, s, WIDTH_B), bf16),
        scratch_shapes=[pltpu.VMEM((SB_HEADS_PER_STEP, SB_Q, HEAD_DIM), f32),
                        pltpu.VMEM((SB_HEADS_PER_STEP, SB_Q, HEAD_DIM), f32)],
        compiler_params=pltpu.CompilerParams(
            dimension_semantics=("parallel", "parallel", "arbitrary"), vmem_limit_bytes=VMEM_LIMIT),
        name="stick_breaking_attn",
    )(main3, main3, main3, upper2)
    return o.reshape(b * s, WIDTH_B)


def _layer_norm(u, g, b):
    mu = jnp.mean(u, axis=-1, keepdims=True)
    d = u - mu
    var = jnp.mean(d * d, axis=-1, keepdims=True)
    return d * lax.rsqrt(var + LN_EPS) * g + b


def _mix_kernel(o0_ref, o1_ref, o2_ref, l0_ref, l1_ref, l2_ref, ob_ref, ga_ref, gb_ref, h_ref,
                pa_ref, pb_ref, wo_ref, lng_ref, lnb_ref, wr_ref, br_ref,
                h1_ref, route_ref, cnt_ref, carry_ref, *, alpha):
    step = pl.program_id(0)

    @pl.when(step == 0)
    def _():
        carry_ref[...] = jnp.zeros_like(carry_ref)

    l0, l1, l2 = l0_ref[...], l1_ref[...], l2_ref[...]
    m = jnp.maximum(jnp.maximum(l0, l1), l2)
    e0, e1, e2 = jnp.exp(l0 - m), jnp.exp(l1 - m), jnp.exp(l2 - m)
    o_a = (e0 * o0_ref[...].astype(f32) + e1 * o1_ref[...].astype(f32)
           + e2 * o2_ref[...].astype(f32)) / (e0 + e1 + e2)
    ya = jnp.dot(o_a.astype(bf16), pa_ref[...], preferred_element_type=f32)
    yb = jnp.dot(ob_ref[...], pb_ref[...], preferred_element_type=f32)
    merged = (jax.nn.sigmoid(ga_ref[...].astype(f32)) * ya
              + jax.nn.sigmoid(gb_ref[...].astype(f32)) * yb)
    y = jnp.dot(merged.astype(bf16), wo_ref[...], preferred_element_type=f32)
    h1 = _layer_norm(alpha * h_ref[...] + y, lng_ref[...], lnb_ref[...])
    d_model = h1.shape[1]
    h1_ref[:, :d_model] = h1
    h1b = h1.astype(bf16)

    tm = h1.shape[0]
    logits = jnp.dot(h1b, wr_ref[...], preferred_element_type=f32) + br_ref[...]
    lane = lax.broadcasted_iota(jnp.int32, (tm, LANES), 1).astype(f32)
    no_lane = float(LANES)
    lc = jnp.where(lane < N_EXPERT_GROUPS, logits, NEG_BIG)
    mc = jnp.max(lc, axis=-1, keepdims=True)
    p_grp = 1.0 / jnp.sum(jnp.exp(lc - mc), axis=-1, keepdims=True)
    g_idx = jnp.min(jnp.where(lc == mc, lane, no_lane), axis=-1, keepdims=True)
    lo_lane = ROUTER_LANE0 + EXPERTS_PER_GROUP * g_idx
    sel = (lane >= lo_lane) & (lane < lo_lane + EXPERTS_PER_GROUP)
    lf = jnp.where(sel, logits, NEG_BIG)
    v1 = jnp.max(lf, axis=-1, keepdims=True)
    i1 = jnp.min(jnp.where(lf == v1, lane, no_lane), axis=-1, keepdims=True)
    lf2 = jnp.where(lane == i1, NEG_BIG, lf)
    v2 = jnp.max(lf2, axis=-1, keepdims=True)
    i2 = jnp.min(jnp.where((lf2 == v2) & sel & (lane != i1), lane, no_lane), axis=-1, keepdims=True)
    t = jnp.exp(v2 - v1)
    w1 = 1.0 / (1.0 + t)
    w2 = t / (1.0 + t)
    e1 = i1 - lo_lane
    e2 = i2 - lo_lane
    a = jnp.minimum(e1, e2)
    b = jnp.maximum(e1, e2)
    pair = a * (2.0 * EXPERTS_PER_GROUP - 1.0 - a) * 0.5 + (b - a - 1.0)
    bucket = PAIRS_PER_GROUP * g_idx + pair
    first_is_a = e1 < e2
    g1 = p_grp * w1
    g2 = p_grp * w2
    w_a = jnp.where(first_is_a, g1, g2)
    w_b = jnp.where(first_is_a, g2, g1)
    h1_ref[:, d_model:] = jnp.where(lane == 0, w_a, jnp.where(lane == 1, w_b, 0.0))

    hit = lane == bucket
    onehot = jnp.where(hit, 1.0, 0.0)
    r = lax.broadcasted_iota(jnp.int32, (tm, tm), 0)
    c = lax.broadcasted_iota(jnp.int32, (tm, tm), 1)
    lower = jnp.where(c < r, 1.0, 0.0).astype(bf16)
    before = jnp.dot(lower, onehot.astype(bf16), preferred_element_type=f32) + carry_ref[0:1, :]
    rank = jnp.sum(jnp.where(hit, before, 0.0), axis=-1, keepdims=True)
    route = jnp.where(lane == 0, bucket, jnp.where(lane == 1, rank, 0.0))
    route_ref[...] = route.astype(jnp.int32)
    new_carry = carry_ref[0:1, :] + jnp.sum(onehot, axis=0, keepdims=True)
    carry_ref[...] = jnp.broadcast_to(new_carry, carry_ref.shape)
    cnt_ref[...] = jnp.broadcast_to(new_carry, cnt_ref.shape)


def _mix(outs, lses, o_b, main2, h, pa, pb, wo, lng, lnb, wr, br, col, alpha, tm):
    t = main2.shape[0]
    d = h.shape[1]
    row = lambda width, c: pl.BlockSpec((tm, width), lambda i: (i, c))
    const = lambda shape: pl.BlockSpec(shape, lambda i: (0, 0), pipeline_mode=pl.Buffered(1))
    in_specs = ([row(GROUP_W, 0)] * 6
                + [row(WIDTH_B, 0), row(d, col["ga"] // d), row(d, col["gb"] // d), row(d, 0),
                   const(pa.shape), const(pb.shape), const(wo.shape),
                   const(lng.shape), const(lnb.shape), const(wr.shape), const(br.shape)])
    return pl.pallas_call(
        functools.partial(_mix_kernel, alpha=alpha),
        grid=(t // tm,),
        in_specs=in_specs,
        out_specs=[row(d + LANES, 0), row(LANES, 0),
                   pl.BlockSpec((8, LANES), lambda i: (0, 0))],
        out_shape=[jax.ShapeDtypeStruct((t, d + LANES), f32),
                   jax.ShapeDtypeStruct((t, LANES), jnp.int32),
                   jax.ShapeDtypeStruct((8, LANES), f32)],
        scratch_shapes=[pltpu.VMEM((8, LANES), f32)],
        compiler_params=pltpu.CompilerParams(
            dimension_semantics=("arbitrary",), vmem_limit_bytes=VMEM_LIMIT),
        name="mix_ln_router",
    )(*outs, *lses, o_b, main2, main2, h, pa, pb, wo, lng, lnb, wr, br)


def _gather_rows(idx_ref, n_rows, src_hbm, dst, sem):
    def body(r, _):
        pltpu.make_async_copy(src_hbm.at[pl.ds(idx_ref[0, 0, r], 1)], dst.at[pl.ds(r, 1)], sem).start()
        return 0
    lax.fori_loop(0, n_rows, body, 0, unroll=8)


def _scatter_rows(idx_ref, n_rows, src, dst_hbm, sem):
    def body(r, _):
        pltpu.make_async_copy(src.at[pl.ds(r, 1)], dst_hbm.at[pl.ds(idx_ref[0, 0, r], 1)], sem).start()
        return 0
    lax.fori_loop(0, n_rows, body, 0, unroll=8)


def _expert_kernel(ta_ref, tb_ref, nu_ref, src_cur, src_nxt, dst_cur, x_hbm,
                   wua_ref, wub_ref, wda_ref, wdb_ref, lng_ref, lnb_ref, o_hbm,
                   xbuf, obuf, gsem, ssem, *, alpha):
    i = pl.program_id(0)
    n = pl.num_programs(0)
    slot = i % 2
    d = obuf.shape[2]
    ff = wda_ref.shape[1]

    @pl.when(i == 0)
    def _():
        _gather_rows(src_cur, EXPERT_TILE, x_hbm, xbuf.at[0], gsem.at[0])

    @pl.when(i + 1 < n)
    def _():
        _gather_rows(src_nxt, EXPERT_TILE, x_hbm, xbuf.at[1 - slot], gsem.at[1 - slot])

    pltpu.make_async_copy(x_hbm.at[pl.ds(0, EXPERT_TILE)], xbuf.at[slot], gsem.at[slot]).wait()

    @pl.when(i >= 2)
    def _():
        pltpu.make_async_copy(obuf.at[slot], o_hbm.at[pl.ds(0, EXPERT_TILE)], ssem.at[slot]).wait()

    @pl.when(i < nu_ref[0])
    def _():
        x = xbuf[slot, :, :d]
        gates = xbuf[slot, :, d:]
        xb = x.astype(bf16)

        def expert(wu_ref, wd_ref):
            hh = jnp.dot(xb, wu_ref[0], preferred_element_type=f32)
            a, b = hh[:, :ff], hh[:, ff:]
            act = a * jax.nn.sigmoid(a) * b
            return jnp.dot(act.astype(bf16), wd_ref[0], preferred_element_type=f32)

        y = gates[:, 0:1] * expert(wua_ref, wda_ref) + gates[:, 1:2] * expert(wub_ref, wdb_ref)
        obuf[slot] = _layer_norm(alpha * x + y, lng_ref[...], lnb_ref[...])

    _scatter_rows(dst_cur, EXPERT_TILE, obuf.at[slot], o_hbm, ssem.at[slot])

    @pl.when(i == n - 1)
    def _():
        pltpu.make_async_copy(obuf.at[slot], o_hbm.at[pl.ds(0, EXPERT_TILE)], ssem.at[slot]).wait()
        pltpu.make_async_copy(obuf.at[1 - slot], o_hbm.at[pl.ds(0, EXPERT_TILE)],
                              ssem.at[1 - slot]).wait()


def _experts(h1x, src, dst, tile_a, tile_b, n_used, wu, wd, lng, lnb, alpha, n_rows_out):
    d = h1x.shape[1] - LANES
    n_tiles = src.shape[0]
    ff2 = wu.shape[2]
    ff = wd.shape[1]
    idx_spec = lambda f: pl.BlockSpec((1, 1, EXPERT_TILE), f, memory_space=pltpu.SMEM)
    const = lambda shape: pl.BlockSpec(shape, lambda i, ta, tb, nu: (0, 0))
    grid_spec = pltpu.PrefetchScalarGridSpec(
        num_scalar_prefetch=3,
        grid=(n_tiles,),
        in_specs=[
            idx_spec(lambda i, ta, tb, nu: (i, 0, 0)),
            idx_spec(lambda i, ta, tb, nu: (jnp.minimum(i + 1, n_tiles - 1), 0, 0)),
            idx_spec(lambda i, ta, tb, nu: (i, 0, 0)),
            pl.BlockSpec(memory_space=pl.ANY),
            pl.BlockSpec((1, d, ff2), lambda i, ta, tb, nu: (ta[i], 0, 0)),
            pl.BlockSpec((1, d, ff2), lambda i, ta, tb, nu: (tb[i], 0, 0)),
            pl.BlockSpec((1, ff, d), lambda i, ta, tb, nu: (ta[i], 0, 0)),
            pl.BlockSpec((1, ff, d), lambda i, ta, tb, nu: (tb[i], 0, 0)),
            const(lng.shape), const(lnb.shape),
        ],
        out_specs=pl.BlockSpec(memory_space=pl.ANY),
        scratch_shapes=[pltpu.VMEM((2, EXPERT_TILE, d + LANES), f32),
                        pltpu.VMEM((2, EXPERT_TILE, d), f32),
                        pltpu.SemaphoreType.DMA((2,)), pltpu.SemaphoreType.DMA((2,))],
    )
    return pl.pallas_call(
        functools.partial(_expert_kernel, alpha=alpha),
        grid_spec=grid_spec,
        out_shape=jax.ShapeDtypeStruct((n_rows_out, d), f32),
        compiler_params=pltpu.CompilerParams(
            dimension_semantics=("arbitrary",), vmem_limit_bytes=VMEM_LIMIT),
        name="experts_ln",
    )(tile_a, tile_b, n_used, src, src, dst, h1x, wu, wu, wd, wd, lng, lnb)


def _rope_tables(positions, dilation):
    b, s = positions.shape
    half = ROPE_DIM // 2
    inv_freq = ROPE_THETA ** (-2.0 * jnp.arange(half, dtype=f32) / ROPE_DIM)
    ang = positions.astype(f32)[:, :, None] * inv_freq
    cos, sin = jnp.cos(ang), jnp.sin(ang)
    pad = LANES - ROPE_DIM
    cos_f = jnp.concatenate([cos, cos, jnp.ones((b, s, pad), f32)], axis=-1)
    sin_f = jnp.concatenate([-sin, sin, jnp.zeros((b, s, pad), f32)], axis=-1)
    regroup = lambda a: a.reshape(b, s // dilation, dilation, LANES).transpose(0, 2, 1, 3)
    return regroup(cos_f), regroup(sin_f)


def _bucket_experts():
    a_ids, b_ids = [], []
    for g in range(N_EXPERT_GROUPS):
        for a in range(EXPERTS_PER_GROUP):
            for b in range(a + 1, EXPERTS_PER_GROUP):
                a_ids.append(g * EXPERTS_PER_GROUP + a)
                b_ids.append(g * EXPERTS_PER_GROUP + b)
    return jnp.array(a_ids, jnp.int32), jnp.array(b_ids, jnp.int32)


def _route_plan(route, cnt, t, n_tiles):
    bucket = route[:, 0]
    rank = route[:, 1]
    counts = cnt[0, :N_BUCKETS].astype(jnp.int32)
    padded = (counts + EXPERT_TILE - 1) // EXPERT_TILE * EXPERT_TILE
    ends = jnp.cumsum(padded)
    offs = ends - padded
    pos = offs[bucket] + rank
    n_used = (ends[-1] // EXPERT_TILE).astype(jnp.int32)
    tile_start = jnp.arange(n_tiles, dtype=jnp.int32) * EXPERT_TILE
    tile_bucket = jnp.sum((tile_start[:, None] >= ends[None, :]).astype(jnp.int32), axis=1)
    tile_bucket = jnp.minimum(tile_bucket, N_BUCKETS - 1)
    last_used = tile_bucket[jnp.maximum(n_used - 1, 0)]
    tile_bucket = jnp.where(jnp.arange(n_tiles) < n_used, tile_bucket, last_used)
    a_ids, b_ids = _bucket_experts()
    n_slots = n_tiles * EXPERT_TILE
    src = jnp.zeros((n_slots,), jnp.int32).at[pos].set(jnp.arange(t, dtype=jnp.int32))
    slot = jnp.arange(n_slots, dtype=jnp.int32)
    slot_bucket = jnp.repeat(tile_bucket, EXPERT_TILE)
    valid = (slot < ends[-1]) & (slot - offs[slot_bucket] < counts[slot_bucket])
    dummy = t + jnp.cumsum(jnp.where(valid, 0, 1)) - 1
    dst = jnp.where(valid, src, dummy).astype(jnp.int32)
    shape = (n_tiles, 1, EXPERT_TILE)
    return (src.reshape(shape), dst.reshape(shape), a_ids[tile_bucket], b_ids[tile_bucket],
            n_used.reshape(1))


def kernel(x, positions, w_in, p_a, p_b, w_o, ln_g, ln_b, router_coarse, router_coarse_bias,
           router_fine, router_fine_bias, w_up, w_down):
    b, s, d = x.shape
    depth = w_in.shape[0]
    t = b * s
    alpha = float((2 * depth) ** 0.25)
    assert s % (DIL_CONFIGS[-1][1] * BLK) == 0 and s % SB_Q == 0 and d % GROUP_W == 0

    sizes = (("ga", d), ("gb", d), ("qb", WIDTH_B), ("kb", WIDTH_B), ("vb", WIDTH_B))
    col, off = {}, 0
    for name, width in sizes:
        col[name] = off
        off += width
    a3, b3 = 3 * WIDTH_A, 3 * WIDTH_B

    tables = [_rope_tables(positions, dilation) for _, dilation in DIL_CONFIGS]
    tm_proj = min(2048, s)
    tm_mix = min(256, t)
    n_tiles = t // EXPERT_TILE + N_BUCKETS

    h = x.reshape(t, d)
    hb = h.astype(bf16)
    for l in range(depth):
        gate_blocks = 2 * d // GROUP_W
        main_block = lambda j: jnp.where(j < gate_blocks, (a3 + b3) // GROUP_W + j,
                                         a3 // GROUP_W + j - gate_blocks)
        main = _in_proj(hb, w_in, l, main_block, 2 * d + b3, b, s, 1, tm_proj, GROUP_W)
        main2 = main.reshape(t, main.shape[-1])
        outs, lses = [], []
        for g, (_, dilation) in enumerate(DIL_CONFIGS):
            group_block = lambda j, g=g: j * N_DIL_GROUPS + g
            qkv = _in_proj(hb, w_in, l, group_block, 3 * GROUP_W, b, s, dilation, tm_proj, GROUP_W)
            o, lse = _dilated_attention(qkv, tables[g][0], tables[g][1], dilation)
            outs.append(o)
            lses.append(lse)
        o_b = _stick_breaking(main.reshape(b, s, main.shape[-1]), col)

        wr = jnp.zeros((d, LANES), f32)
        wr = wr.at[:, :N_EXPERT_GROUPS].set(router_coarse[l])
        wr = wr.at[:, ROUTER_LANE0:ROUTER_LANE0 + N_EXPERTS].set(router_fine[l]).astype(bf16)
        br = jnp.zeros((1, LANES), f32)
        br = br.at[0, :N_EXPERT_GROUPS].set(router_coarse_bias[l])
        br = br.at[0, ROUTER_LANE0:ROUTER_LANE0 + N_EXPERTS].set(router_fine_bias[l])
        h1x, route, cnt = _mix(
            outs, lses, o_b, main2, h, p_a[l].astype(bf16), p_b[l].astype(bf16), w_o[l].astype(bf16),
            ln_g[l, 0].reshape(1, d), ln_b[l, 0].reshape(1, d), wr, br, col, alpha, tm_mix)

        src, dst, tile_a, tile_b, n_used = _route_plan(route, cnt, t, n_tiles)
        h = _experts(h1x, src, dst, tile_a, tile_b, n_used, w_up[l].astype(bf16), w_down[l].astype(bf16),
                     ln_g[l, 1].reshape(1, d), ln_b[l, 1].reshape(1, d), alpha, n_tiles * EXPERT_TILE)
        hb = h[:t].astype(bf16)
    return h[:t].reshape(b, s, d)
```

```python
import functools

import jax
import jax.numpy as jnp
from jax import lax
from jax.experimental import pallas as pl
from jax.experimental.pallas import tpu as pltpu

HEAD_DIM = 128
DIL_CONFIGS = ((128, 1), (512, 4), (2048, 16))
N_DIL_GROUPS = 3
HEADS_PER_DIL_GROUP = 4
N_SB_HEADS = 8
ROPE_THETA = 500000.0
ROPE_DIM = HEAD_DIM // 4
N_EXPERT_GROUPS = 4
EXPERTS_PER_GROUP = 4
N_EXPERTS = N_EXPERT_GROUPS * EXPERTS_PER_GROUP
PAIRS_PER_GROUP = EXPERTS_PER_GROUP * (EXPERTS_PER_GROUP - 1) // 2
N_BUCKETS = N_EXPERT_GROUPS * PAIRS_PER_GROUP
LN_EPS = 1e-5

WIDTH_A = N_DIL_GROUPS * HEADS_PER_DIL_GROUP * HEAD_DIM
GROUP_W = HEADS_PER_DIL_GROUP * HEAD_DIM
WIDTH_B = N_SB_HEADS * HEAD_DIM

BLK = 128
LANES = 128
SUBLANES = 8
DIL_RES_PER_STEP = 4
SB_Q = 256
SB_K = 256
SB_HEADS_PER_STEP = 4
LOG2_E = 1.4426950408889634
SB_DEAD_LOG2 = -151.0
ROUTER_LANE0 = N_EXPERT_GROUPS
EXPERT_TILE = 256
VMEM_LIMIT = 56 * 1024 * 1024
NEG_BIG = -1e30

f32 = jnp.float32
bf16 = jnp.bfloat16


def _in_proj_kernel(x_ref, w_ref, o_ref, *scratch, dilation):
    acc = jnp.dot(x_ref[...], w_ref[0].astype(bf16), preferred_element_type=f32)
    if dilation == 1:
        o_ref[0, 0] = acc.astype(o_ref.dtype)
        return
    (acc_ref,) = scratch
    n = acc.shape[0] // dilation
    for s in range(acc.shape[1] // LANES):
        acc_ref[s] = acc[:, s * LANES:(s + 1) * LANES]
    for r in range(dilation):
        for s in range(acc.shape[1] // LANES):
            o_ref[0, r, :, s * LANES:(s + 1) * LANES] = (
                acc_ref[s, pl.ds(r, n, stride=dilation), :].astype(o_ref.dtype))


def _in_proj(x, w_in, layer, col_block, n_out, b, s, dilation, tm, tn):
    k = x.shape[1]
    per_b = s // tm
    scratch = [] if dilation == 1 else [pltpu.VMEM((tn // LANES, tm, LANES), f32)]
    return pl.pallas_call(
        functools.partial(_in_proj_kernel, dilation=dilation),
        grid=(b * per_b, n_out // tn),
        in_specs=[pl.BlockSpec((tm, k), lambda i, j: (i, 0)),
                  pl.BlockSpec((1, k, tn), lambda i, j: (layer, 0, col_block(j)))],
        out_specs=pl.BlockSpec((1, dilation, tm // dilation, tn),
                               lambda i, j: (i // per_b, 0, i % per_b, j)),
        out_shape=jax.ShapeDtypeStruct((b, dilation, s // dilation, n_out), bf16),
        scratch_shapes=scratch,
        compiler_params=pltpu.CompilerParams(
            dimension_semantics=("parallel", "parallel"), vmem_limit_bytes=VMEM_LIMIT),
        name=f"in_proj_d{dilation}",
    )(x, w_in)


def _rotate(t, cos_f, sin_f, lane):
    swapped = jnp.where(lane < ROPE_DIM // 2,
                        pltpu.roll(t, LANES - ROPE_DIM // 2, 1),
                        pltpu.roll(t, ROPE_DIM // 2, 1))
    return t * cos_f + swapped * sin_f


def _dil_kernel(q_ref, kc_ref, kp_ref, vc_ref, vp_ref, cc_ref, sc_ref, cp_ref, sp_ref,
                o_ref, lse_ref, o_scr, lse_scr, *, dilation, res_per_step):
    n = pl.program_id(1)
    rr = pl.program_id(2)
    lane = lax.broadcasted_iota(jnp.int32, (BLK, LANES), 1)
    qi = lax.broadcasted_iota(jnp.int32, (BLK, 2 * BLK), 0)
    kj = lax.broadcasted_iota(jnp.int32, (BLK, 2 * BLK), 1)
    first_key = jnp.where(n > 0, 0, BLK)
    valid = (kj >= jnp.maximum(qi, first_key)) & (kj <= qi + BLK)
    scale = HEAD_DIM ** -0.5

    def residue(ri, _):
        cos_c, sin_c, cos_p, sin_p = cc_ref[0, ri], sc_ref[0, ri], cp_ref[0, ri], sp_ref[0, ri]
        rows = pl.ds(rr * res_per_step + ri, BLK, stride=dilation) if dilation > 1 else slice(None)
        for h in range(HEADS_PER_DIL_GROUP):
            sl = slice(h * HEAD_DIM, (h + 1) * HEAD_DIM)
            q = _rotate(q_ref[0, ri, :, sl].astype(f32), cos_c, sin_c, lane).astype(bf16)
            kc = _rotate(kc_ref[0, ri, :, sl].astype(f32), cos_c, sin_c, lane).astype(bf16)
            kp = _rotate(kp_ref[0, ri, :, sl].astype(f32), cos_p, sin_p, lane).astype(bf16)
            k = jnp.concatenate([kp, kc], axis=0)
            v = jnp.concatenate([vp_ref[0, ri, :, sl], vc_ref[0, ri, :, sl]], axis=0)
            s = lax.dot_general(q, k, (((1,), (1,)), ((), ())), preferred_element_type=f32) * scale
            s = jnp.where(valid, s, NEG_BIG)
            m = jnp.max(s, axis=-1, keepdims=True)
            p = jnp.exp(s - m)
            den = jnp.sum(p, axis=-1, keepdims=True)
            o_scr[h, rows, :] = jnp.dot(p.astype(bf16), v, preferred_element_type=f32) / den
            lse_scr[h, rows, :] = jnp.broadcast_to(m + jnp.log(den), (BLK, HEAD_DIM))
        return 0

    lax.fori_loop(0, res_per_step, residue, 0)

    @pl.when(rr == pl.num_programs(2) - 1)
    def _():
        for h in range(HEADS_PER_DIL_GROUP):
            sl = slice(h * HEAD_DIM, (h + 1) * HEAD_DIM)
            o_ref[0, :, sl] = o_scr[h].astype(o_ref.dtype)
            lse_ref[0, :, sl] = lse_scr[h]


def _dilated_attention(qkv, cos_r, sin_r, dilation):
    b, _, sub, _ = qkv.shape
    s = sub * dilation
    nb = sub // BLK
    rs = min(dilation, DIL_RES_PER_STEP)

    def cur(c):
        return pl.BlockSpec((1, rs, BLK, GROUP_W), lambda bi, n, rr: (bi, rr, n, c))

    def prev(c):
        return pl.BlockSpec((1, rs, BLK, GROUP_W), lambda bi, n, rr: (bi, rr, jnp.maximum(n - 1, 0), c))

    tab_cur = pl.BlockSpec((1, rs, BLK, LANES), lambda bi, n, rr: (bi, rr, n, 0))
    tab_prev = pl.BlockSpec((1, rs, BLK, LANES), lambda bi, n, rr: (bi, rr, jnp.maximum(n - 1, 0), 0))
    out_spec = pl.BlockSpec((1, dilation * BLK, GROUP_W), lambda bi, n, rr: (bi, n, 0))
    o, lse = pl.pallas_call(
        functools.partial(_dil_kernel, dilation=dilation, res_per_step=rs),
        grid=(b, nb, dilation // rs),
        in_specs=[cur(0), cur(1), prev(1), cur(2), prev(2), tab_cur, tab_cur, tab_prev, tab_prev],
        out_specs=[out_spec, out_spec],
        out_shape=[jax.ShapeDtypeStruct((b, s, GROUP_W), bf16),
                   jax.ShapeDtypeStruct((b, s, GROUP_W), f32)],
        scratch_shapes=[pltpu.VMEM((HEADS_PER_DIL_GROUP, dilation * BLK, HEAD_DIM), f32),
                        pltpu.VMEM((HEADS_PER_DIL_GROUP, dilation * BLK, HEAD_DIM), f32)],
        compiler_params=pltpu.CompilerParams(
            dimension_semantics=("parallel", "arbitrary", "arbitrary"), vmem_limit_bytes=VMEM_LIMIT),
        name=f"dilated_attn_d{dilation}",
    )(qkv, qkv, qkv, qkv, qkv, cos_r, sin_r, cos_r, sin_r)
    return o.reshape(b * s, GROUP_W), lse.reshape(b * s, GROUP_W)


def _sb_kernel(q_ref, k_ref, v_ref, up_ref, o_ref, acc_ref, c_ref):
    i = pl.program_id(2)
    scale2 = HEAD_DIM ** -0.5 * LOG2_E
    row = lax.broadcasted_iota(jnp.int32, (SB_Q, SB_K), 0)
    coli = lax.broadcasted_iota(jnp.int32, (SB_Q, SB_K), 1)
    causal = coli < row
    acc_ref[...] = jnp.zeros_like(acc_ref)
    c_ref[...] = jnp.zeros_like(c_ref)

    def chunk(j, diag):
        start = pl.multiple_of(j * SB_K, SB_K)
        heads = range(SB_HEADS_PER_STEP)
        sls = [slice(hd * HEAD_DIM, (hd + 1) * HEAD_DIM) for hd in heads]
        ts = [lax.dot_general(q_ref[0, :, sl], k_ref[0, pl.ds(start, SB_K), sl],
                              (((1,), (1,)), ((), ())), preferred_element_type=f32) * scale2
              for sl in sls]
        lss, lks, splits = [], [], []
        for t in ts:
            ls = jnp.minimum(t, 0.0) - jnp.log2(1.0 + jnp.exp2(-jnp.abs(t)))
            lk = ls - t
            if diag:
                lk = jnp.where(causal, lk, 0.0)
            hi = lk.astype(bf16)
            lo = (lk - hi.astype(f32)).astype(bf16)
            lss.append(ls)
            lks.append(lk)
            splits.append(jnp.concatenate([hi, lo], axis=1))
        laters = [jnp.dot(x, up_ref[...], preferred_element_type=f32) for x in splits]
        probs = []
        for hd in heads:
            c = c_ref[hd]
            a = jnp.exp2(lss[hd] + laters[hd] + jnp.concatenate([c, c], axis=1))
            if diag:
                a = jnp.where(causal, a, 0.0)
            probs.append(a.astype(bf16))
            c_ref[hd] = c + jnp.sum(lks[hd], axis=1, keepdims=True)
        for hd in heads:
            acc_ref[hd] += jnp.dot(probs[hd], v_ref[0, pl.ds(start, SB_K), sls[hd]],
                                   preferred_element_type=f32)

    chunk(i, True)

    def live():
        return jnp.max(c_ref[...]) > SB_DEAD_LOG2

    def cond(state):
        t, go = state
        return jnp.logical_and(t < i, go)

    def body(state):
        t, _ = state
        chunk(i - 1 - t, False)
        return t + 1, live()

    lax.while_loop(cond, body, (jnp.int32(0), live()))
    for hd in range(SB_HEADS_PER_STEP):
        o_ref[0, :, hd * HEAD_DIM:(hd + 1) * HEAD_DIM] = acc_ref[hd].astype(o_ref.dtype)


def _stick_breaking(main3, col):
    b, s, _ = main3.shape
    w = SB_HEADS_PER_STEP * HEAD_DIM
    qb, kb, vb = col["qb"] // w, col["kb"] // w, col["vb"] // w
    jr = lax.broadcasted_iota(jnp.int32, (2 * SB_K, SB_K), 0)
    sc = lax.broadcasted_iota(jnp.int32, (2 * SB_K, SB_K), 1)
    upper2 = jnp.where((jr % SB_K) > sc, 1.0, 0.0).astype(bf16)
    o = pl.pallas_call(
        _sb_kernel,
        grid=(b, N_SB_HEADS // SB_HEADS_PER_STEP, s // SB_Q),
        in_specs=[pl.BlockSpec((1, SB_Q, w), lambda bi, h, i: (bi, i, qb + h)),
                  pl.BlockSpec((1, s, w), lambda bi, h, i: (bi, 0, kb + h)),
                  pl.BlockSpec((1, s, w), lambda bi, h, i: (bi, 0, vb + h)),
                  pl.BlockSpec((2 * SB_K, SB_K), lambda bi, h, i: (0, 0))],
        out_specs=pl.BlockSpec((1, SB_Q, w), lambda bi, h, i: (bi, i, h)),
        out_shape=jax.ShapeDtypeStruct((b, s, WIDTH_B), bf16),
        scratch_shapes=[pltpu.VMEM((SB_HEADS_PER_STEP, SB_Q, HEAD_DIM), f32),
                        pltpu.VMEM((SB_HEADS_PER_STEP, SB_Q, HEAD_DIM), f32)],
        compiler_params=pltpu.CompilerParams(
            dimension_semantics=("parallel", "parallel", "arbitrary"), vmem_limit_bytes=VMEM_LIMIT),
        name="stick_breaking_attn",
    )(main3, main3, main3, upper2)
    return o.reshape(b * s, WIDTH_B)


def _layer_norm(u, g, b):
    mu = jnp.mean(u, axis=-1, keepdims=True)
    d = u - mu
    var = jnp.mean(d * d, axis=-1, keepdims=True)
    return d * lax.rsqrt(var + LN_EPS) * g + b


def _mix_kernel(o0_ref, o1_ref, o2_ref, l0_ref, l1_ref, l2_ref, ob_ref, ga_ref, gb_ref, h_ref,
                pa_ref, pb_ref, wo_ref, lng_ref, lnb_ref, wr_ref, br_ref,
                h1_ref, route_ref, cnt_ref, carry_ref, *, alpha, h_chunked):
    step = pl.program_id(0)

    @pl.when(step == 0)
    def _():
        carry_ref[...] = jnp.zeros_like(carry_ref)

    l0, l1, l2 = l0_ref[...], l1_ref[...], l2_ref[...]
    m = jnp.maximum(jnp.maximum(l0, l1), l2)
    e0, e1, e2 = jnp.exp(l0 - m), jnp.exp(l1 - m), jnp.exp(l2 - m)
    o_a = (e0 * o0_ref[...].astype(f32) + e1 * o1_ref[...].astype(f32)
           + e2 * o2_ref[...].astype(f32)) / (e0 + e1 + e2)
    ya = jnp.dot(o_a.astype(bf16), pa_ref[...], preferred_element_type=f32)
    yb = jnp.dot(ob_ref[...], pb_ref[...], preferred_element_type=f32)
    merged = (jax.nn.sigmoid(ga_ref[...].astype(f32)) * ya
              + jax.nn.sigmoid(gb_ref[...].astype(f32)) * yb)
    y = jnp.dot(merged.astype(bf16), wo_ref[...], preferred_element_type=f32)
    tm, d_model = y.shape
    n_chunks = d_model // LANES
    if h_chunked:
        h_res = jnp.concatenate(
            [h_ref[pl.ds(c, tm, stride=n_chunks), :] for c in range(n_chunks)], axis=1)
    else:
        h_res = h_ref[...]
    h1 = _layer_norm(alpha * h_res + y, lng_ref[...], lnb_ref[...])
    pitch = h1_ref.shape[0] // tm
    for c in range(n_chunks):
        h1_ref[pl.ds(c, tm, stride=pitch), :] = h1[:, c * LANES:(c + 1) * LANES]
    for c in range(n_chunks + 1, pitch):
        h1_ref[pl.ds(c, tm, stride=pitch), :] = jnp.zeros((tm, LANES), f32)
    h1b = h1.astype(bf16)

    logits = jnp.dot(h1b, wr_ref[...], preferred_element_type=f32) + br_ref[...]
    lane = lax.broadcasted_iota(jnp.int32, (tm, LANES), 1).astype(f32)
    no_lane = float(LANES)
    lc = jnp.where(lane < N_EXPERT_GROUPS, logits, NEG_BIG)
    mc = jnp.max(lc, axis=-1, keepdims=True)
    p_grp = 1.0 / jnp.sum(jnp.exp(lc - mc), axis=-1, keepdims=True)
    g_idx = jnp.min(jnp.where(lc == mc, lane, no_lane), axis=-1, keepdims=True)
    lo_lane = ROUTER_LANE0 + EXPERTS_PER_GROUP * g_idx
    sel = (lane >= lo_lane) & (lane < lo_lane + EXPERTS_PER_GROUP)
    lf = jnp.where(sel, logits, NEG_BIG)
    v1 = jnp.max(lf, axis=-1, keepdims=True)
    i1 = jnp.min(jnp.where(lf == v1, lane, no_lane), axis=-1, keepdims=True)
    lf2 = jnp.where(lane == i1, NEG_BIG, lf)
    v2 = jnp.max(lf2, axis=-1, keepdims=True)
    i2 = jnp.min(jnp.where((lf2 == v2) & sel & (lane != i1), lane, no_lane), axis=-1, keepdims=True)
    t = jnp.exp(v2 - v1)
    w1 = 1.0 / (1.0 + t)
    w2 = t / (1.0 + t)
    e1 = i1 - lo_lane
    e2 = i2 - lo_lane
    a = jnp.minimum(e1, e2)
    b = jnp.maximum(e1, e2)
    pair = a * (2.0 * EXPERTS_PER_GROUP - 1.0 - a) * 0.5 + (b - a - 1.0)
    bucket = PAIRS_PER_GROUP * g_idx + pair
    first_is_a = e1 < e2
    g1 = p_grp * w1
    g2 = p_grp * w2
    w_a = jnp.where(first_is_a, g1, g2)
    w_b = jnp.where(first_is_a, g2, g1)
    h1_ref[pl.ds(n_chunks, tm, stride=pitch), :] = jnp.where(
        lane == 0, w_a, jnp.where(lane == 1, w_b, 0.0))

    hit = lane == bucket
    onehot = jnp.where(hit, 1.0, 0.0)
    r = lax.broadcasted_iota(jnp.int32, (tm, tm), 0)
    c = lax.broadcasted_iota(jnp.int32, (tm, tm), 1)
    lower = jnp.where(c < r, 1.0, 0.0).astype(bf16)
    before = jnp.dot(lower, onehot.astype(bf16), preferred_element_type=f32) + carry_ref[0:1, :]
    rank = jnp.sum(jnp.where(hit, before, 0.0), axis=-1, keepdims=True)
    route = jnp.where(lane == 0, bucket, jnp.where(lane == 1, rank, 0.0))
    route_ref[...] = route.astype(jnp.int32)
    new_carry = carry_ref[0:1, :] + jnp.sum(onehot, axis=0, keepdims=True)
    carry_ref[...] = jnp.broadcast_to(new_carry, carry_ref.shape)
    cnt_ref[...] = jnp.broadcast_to(new_carry, cnt_ref.shape)


def _chunk_pitch(d):
    return -(-(d // LANES + 1) // SUBLANES) * SUBLANES


def _mix(outs, lses, o_b, main2, h, h_chunked, pa, pb, wo, lng, lnb, wr, br, col, alpha, tm):
    t = main2.shape[0]
    d = pa.shape[1]
    pitch = _chunk_pitch(d)
    row = lambda width, c: pl.BlockSpec((tm, width), lambda i: (i, c))
    const = lambda shape: pl.BlockSpec(shape, lambda i: (0, 0), pipeline_mode=pl.Buffered(1))
    h_spec = pl.BlockSpec((tm * (d // LANES), LANES), lambda i: (i, 0)) if h_chunked else row(d, 0)
    in_specs = ([row(GROUP_W, 0)] * 6
                + [row(WIDTH_B, 0), row(d, col["ga"] // d), row(d, col["gb"] // d), h_spec,
                   const(pa.shape), const(pb.shape), const(wo.shape),
                   const(lng.shape), const(lnb.shape), const(wr.shape), const(br.shape)])
    return pl.pallas_call(
        functools.partial(_mix_kernel, alpha=alpha, h_chunked=h_chunked),
        grid=(t // tm,),
        in_specs=in_specs,
        out_specs=[pl.BlockSpec((tm * pitch, LANES), lambda i: (i, 0)), row(LANES, 0),
                   pl.BlockSpec((8, LANES), lambda i: (0, 0))],
        out_shape=[jax.ShapeDtypeStruct((t * pitch, LANES), f32),
                   jax.ShapeDtypeStruct((t, LANES), jnp.int32),
                   jax.ShapeDtypeStruct((8, LANES), f32)],
        scratch_shapes=[pltpu.VMEM((8, LANES), f32)],
        compiler_params=pltpu.CompilerParams(
            dimension_semantics=("arbitrary",), vmem_limit_bytes=VMEM_LIMIT),
        name="mix_ln_router",
    )(*outs, *lses, o_b, main2, main2, h, pa, pb, wo, lng, lnb, wr, br)


def _gather_tokens(idx_ref, n_tok, pitch, src_hbm, dst, sem):
    def body(p, _):
        for k in range(2):
            tok = 2 * p + k
            start = pl.multiple_of(idx_ref[0, 0, tok], SUBLANES)
            pltpu.make_async_copy(src_hbm.at[pl.ds(start, pitch)],
                                  dst.at[pl.ds(pl.multiple_of(tok * pitch, SUBLANES), pitch)],
                                  sem).start(priority=k)
        return 0
    lax.fori_loop(0, n_tok // 2, body, 0, unroll=4)


def _scatter_tokens(idx_ref, n_tok, pitch, src, dst_hbm, sem):
    def body(p, _):
        for k in range(2):
            tok = 2 * p + k
            start = pl.multiple_of(idx_ref[0, 0, tok], SUBLANES)
            pltpu.make_async_copy(src.at[pl.ds(pl.multiple_of(tok * pitch, SUBLANES), pitch)],
                                  dst_hbm.at[pl.ds(start, pitch)], sem).start(priority=k)
        return 0
    lax.fori_loop(0, n_tok // 2, body, 0, unroll=4)


def _expert_kernel(ta_ref, tb_ref, nu_ref, src_cur, src_nxt, dst_cur, x_hbm,
                   wua_ref, wub_ref, wda_ref, wdb_ref, lng_ref, lnb_ref, o_hbm,
                   xbuf, obuf, gsem, ssem, *, alpha, pitch):
    i = pl.program_id(0)
    n = pl.num_programs(0)
    slot = i % 2
    ff = wda_ref.shape[1]
    d = wda_ref.shape[2]
    n_chunks = d // LANES
    x_rows = EXPERT_TILE * pitch
    o_rows = EXPERT_TILE * n_chunks

    @pl.when(i == 0)
    def _():
        _gather_tokens(src_cur, EXPERT_TILE, pitch, x_hbm, xbuf.at[0], gsem.at[0])

    @pl.when(i + 1 < n)
    def _():
        _gather_tokens(src_nxt, EXPERT_TILE, pitch, x_hbm, xbuf.at[1 - slot], gsem.at[1 - slot])

    pltpu.make_async_copy(x_hbm.at[pl.ds(0, x_rows)], xbuf.at[slot], gsem.at[slot]).wait()

    @pl.when(i >= 2)
    def _():
        pltpu.make_async_copy(obuf.at[slot], o_hbm.at[pl.ds(0, o_rows)], ssem.at[slot]).wait()

    @pl.when(i < nu_ref[0])
    def _():
        chunk = lambda c: xbuf[slot, pl.ds(c, EXPERT_TILE, stride=pitch), :]
        x = jnp.concatenate([chunk(c) for c in range(n_chunks)], axis=1)
        gates = chunk(n_chunks)
        xb = x.astype(bf16)

        def expert(wu_ref, wd_ref):
            hh = jnp.dot(xb, wu_ref[0], preferred_element_type=f32)
            a, b = hh[:, :ff], hh[:, ff:]
            act = a * jax.nn.sigmoid(a) * b
            return jnp.dot(act.astype(bf16), wd_ref[0], preferred_element_type=f32)

        y = gates[:, 0:1] * expert(wua_ref, wda_ref) + gates[:, 1:2] * expert(wub_ref, wdb_ref)
        h2 = _layer_norm(alpha * x + y, lng_ref[...], lnb_ref[...])
        for c in range(n_chunks):
            obuf[slot, pl.ds(c, EXPERT_TILE, stride=n_chunks), :] = h2[:, c * LANES:(c + 1) * LANES]

    _scatter_tokens(dst_cur, EXPERT_TILE, n_chunks, obuf.at[slot], o_hbm, ssem.at[slot])

    @pl.when(i == n - 1)
    def _():
        pltpu.make_async_copy(obuf.at[slot], o_hbm.at[pl.ds(0, o_rows)], ssem.at[slot]).wait()
        pltpu.make_async_copy(obuf.at[1 - slot], o_hbm.at[pl.ds(0, o_rows)], ssem.at[1 - slot]).wait()


def _experts(h1x, src, dst, tile_a, tile_b, n_used, wu, wd, layer, lng, lnb, alpha, n_tok_out):
    d = wd.shape[2]
    pitch = _chunk_pitch(d)
    n_chunks = d // LANES
    n_tiles = src.shape[0]
    ff2 = wu.shape[2]
    ff = wd.shape[1]
    e0 = layer * N_EXPERTS
    idx_spec = lambda f: pl.BlockSpec((1, 1, EXPERT_TILE), f, memory_space=pltpu.SMEM)
    const = lambda shape: pl.BlockSpec(shape, lambda i, ta, tb, nu: (0, 0))
    grid_spec = pltpu.PrefetchScalarGridSpec(
        num_scalar_prefetch=3,
        grid=(n_tiles,),
        in_specs=[
            idx_spec(lambda i, ta, tb, nu: (i, 0, 0)),
            idx_spec(lambda i, ta, tb, nu: (jnp.minimum(i + 1, n_tiles - 1), 0, 0)),
            idx_spec(lambda i, ta, tb, nu: (i, 0, 0)),
            pl.BlockSpec(memory_space=pl.ANY),
            pl.BlockSpec((1, d, ff2), lambda i, ta, tb, nu: (e0 + ta[i], 0, 0)),
            pl.BlockSpec((1, d, ff2), lambda i, ta, tb, nu: (e0 + tb[i], 0, 0)),
            pl.BlockSpec((1, ff, d), lambda i, ta, tb, nu: (e0 + ta[i], 0, 0)),
            pl.BlockSpec((1, ff, d), lambda i, ta, tb, nu: (e0 + tb[i], 0, 0)),
            const(lng.shape), const(lnb.shape),
        ],
        out_specs=pl.BlockSpec(memory_space=pl.ANY),
        scratch_shapes=[pltpu.VMEM((2, EXPERT_TILE * pitch, LANES), f32),
                        pltpu.VMEM((2, EXPERT_TILE * n_chunks, LANES), f32),
                        pltpu.SemaphoreType.DMA((2,)), pltpu.SemaphoreType.DMA((2,))],
    )
    return pl.pallas_call(
        functools.partial(_expert_kernel, alpha=alpha, pitch=pitch),
        grid_spec=grid_spec,
        out_shape=jax.ShapeDtypeStruct((n_tok_out * n_chunks, LANES), f32),
        compiler_params=pltpu.CompilerParams(
            dimension_semantics=("arbitrary",), vmem_limit_bytes=VMEM_LIMIT),
        name="experts_ln",
    )(tile_a, tile_b, n_used, src, src, dst, h1x, wu, wu, wd, wd, lng, lnb)


def _rope_tables(positions, dilation):
    b, s = positions.shape
    half = ROPE_DIM // 2
    inv_freq = ROPE_THETA ** (-2.0 * jnp.arange(half, dtype=f32) / ROPE_DIM)
    ang = positions.astype(f32)[:, :, None] * inv_freq
    cos, sin = jnp.cos(ang), jnp.sin(ang)
    pad = LANES - ROPE_DIM
    cos_f = jnp.concatenate([cos, cos, jnp.ones((b, s, pad), f32)], axis=-1)
    sin_f = jnp.concatenate([-sin, sin, jnp.zeros((b, s, pad), f32)], axis=-1)
    regroup = lambda a: a.reshape(b, s // dilation, dilation, LANES).transpose(0, 2, 1, 3)
    return regroup(cos_f), regroup(sin_f)


def _bucket_experts():
    a_ids, b_ids = [], []
    for g in range(N_EXPERT_GROUPS):
        for a in range(EXPERTS_PER_GROUP):
            for b in range(a + 1, EXPERTS_PER_GROUP):
                a_ids.append(g * EXPERTS_PER_GROUP + a)
                b_ids.append(g * EXPERTS_PER_GROUP + b)
    return jnp.array(a_ids, jnp.int32), jnp.array(b_ids, jnp.int32)


def _route_plan(route, cnt, t, n_tiles):
    bucket = route[:, 0]
    rank = route[:, 1]
    counts = cnt[0, :N_BUCKETS].astype(jnp.int32)
    padded = (counts + EXPERT_TILE - 1) // EXPERT_TILE * EXPERT_TILE
    ends = jnp.cumsum(padded)
    offs = ends - padded
    pos = offs[bucket] + rank
    n_used = (ends[-1] // EXPERT_TILE).astype(jnp.int32)
    tile_start = jnp.arange(n_tiles, dtype=jnp.int32) * EXPERT_TILE
    tile_bucket = jnp.sum((tile_start[:, None] >= ends[None, :]).astype(jnp.int32), axis=1)
    tile_bucket = jnp.minimum(tile_bucket, N_BUCKETS - 1)
    last_used = tile_bucket[jnp.maximum(n_used - 1, 0)]
    tile_bucket = jnp.where(jnp.arange(n_tiles) < n_used, tile_bucket, last_used)
    a_ids, b_ids = _bucket_experts()
    n_slots = n_tiles * EXPERT_TILE
    src = jnp.zeros((n_slots,), jnp.int32).at[pos].set(jnp.arange(t, dtype=jnp.int32))
    slot = jnp.arange(n_slots, dtype=jnp.int32)
    slot_bucket = jnp.repeat(tile_bucket, EXPERT_TILE)
    valid = (slot < ends[-1]) & (slot - offs[slot_bucket] < counts[slot_bucket])
    dummy = t + jnp.cumsum(jnp.where(valid, 0, 1)) - 1
    dst = jnp.where(valid, src, dummy).astype(jnp.int32)
    shape = (n_tiles, 1, EXPERT_TILE)
    return (src.reshape(shape), dst.reshape(shape), a_ids[tile_bucket], b_ids[tile_bucket],
            n_used.reshape(1))


def kernel(x, positions, w_in, p_a, p_b, w_o, ln_g, ln_b, router_coarse, router_coarse_bias,
           router_fine, router_fine_bias, w_up, w_down):
    b, s, d = x.shape
    depth = w_in.shape[0]
    t = b * s
    alpha = float((2 * depth) ** 0.25)
    assert s % (DIL_CONFIGS[-1][1] * BLK) == 0 and s % SB_Q == 0 and d % GROUP_W == 0

    sizes = (("ga", d), ("gb", d), ("qb", WIDTH_B), ("kb", WIDTH_B), ("vb", WIDTH_B))
    col, off = {}, 0
    for name, width in sizes:
        col[name] = off
        off += width
    a3, b3 = 3 * WIDTH_A, 3 * WIDTH_B

    tables = [_rope_tables(positions, dilation) for _, dilation in DIL_CONFIGS]
    tm_proj = min(2048, s)
    tm_mix = min(256, t)
    n_tiles = t // EXPERT_TILE + N_BUCKETS
    n_chunks = d // LANES
    wu_all = w_up.astype(bf16).reshape((depth * N_EXPERTS,) + w_up.shape[2:])
    wd_all = w_down.astype(bf16).reshape((depth * N_EXPERTS,) + w_down.shape[2:])

    h = x.reshape(t, d)
    hb = h.astype(bf16)
    for l in range(depth):
        gate_blocks = 2 * d // GROUP_W
        main_block = lambda j: jnp.where(j < gate_blocks, (a3 + b3) // GROUP_W + j,
                                         a3 // GROUP_W + j - gate_blocks)
        main = _in_proj(hb, w_in, l, main_block, 2 * d + b3, b, s, 1, tm_proj, GROUP_W)
        main2 = main.reshape(t, main.shape[-1])
        outs, lses = [], []
        for g, (_, dilation) in enumerate(DIL_CONFIGS):
            group_block = lambda j, g=g: j * N_DIL_GROUPS + g
            qkv = _in_proj(hb, w_in, l, group_block, 3 * GROUP_W, b, s, dilation, tm_proj, GROUP_W)
            o, lse = _dilated_attention(qkv, tables[g][0], tables[g][1], dilation)
            outs.append(o)
            lses.append(lse)
        o_b = _stick_breaking(main.reshape(b, s, main.shape[-1]), col)

        wr = jnp.zeros((d, LANES), f32)
        wr = wr.at[:, :N_EXPERT_GROUPS].set(router_coarse[l])
        wr = wr.at[:, ROUTER_LANE0:ROUTER_LANE0 + N_EXPERTS].set(router_fine[l]).astype(bf16)
        br = jnp.zeros((1, LANES), f32)
        br = br.at[0, :N_EXPERT_GROUPS].set(router_coarse_bias[l])
        br = br.at[0, ROUTER_LANE0:ROUTER_LANE0 + N_EXPERTS].set(router_fine_bias[l])
        h1x, route, cnt = _mix(
            outs, lses, o_b, main2, h, l > 0, p_a[l].astype(bf16), p_b[l].astype(bf16),
            w_o[l].astype(bf16), ln_g[l, 0].reshape(1, d), ln_b[l, 0].reshape(1, d), wr, br, col,
            alpha, tm_mix)

        src, dst, tile_a, tile_b, n_used = _route_plan(route, cnt, t, n_tiles)
        h = _experts(h1x, src * _chunk_pitch(d), dst * n_chunks, tile_a, tile_b, n_used, wu_all, wd_all,
                     l, ln_g[l, 1].reshape(1, d), ln_b[l, 1].reshape(1, d), alpha,
                     n_tiles * EXPERT_TILE)
        hb = h[:t * n_chunks].reshape(t, d).astype(bf16)
    return h[:t * n_chunks].reshape(b, s, d)
```

```python
import functools

import jax
import jax.numpy as jnp
from jax import lax
from jax.experimental import pallas as pl
from jax.experimental.pallas import tpu as pltpu

HEAD_DIM = 128
DIL_CONFIGS = ((128, 1), (512, 4), (2048, 16))
N_DIL_GROUPS = 3
HEADS_PER_DIL_GROUP = 4
N_SB_HEADS = 8
ROPE_THETA = 500000.0
ROPE_DIM = HEAD_DIM // 4
N_EXPERT_GROUPS = 4
EXPERTS_PER_GROUP = 4
N_EXPERTS = N_EXPERT_GROUPS * EXPERTS_PER_GROUP
PAIRS_PER_GROUP = EXPERTS_PER_GROUP * (EXPERTS_PER_GROUP - 1) // 2
N_BUCKETS = N_EXPERT_GROUPS * PAIRS_PER_GROUP
LN_EPS = 1e-5

WIDTH_A = N_DIL_GROUPS * HEADS_PER_DIL_GROUP * HEAD_DIM
GROUP_W = HEADS_PER_DIL_GROUP * HEAD_DIM
WIDTH_B = N_SB_HEADS * HEAD_DIM

BLK = 128
LANES = 128
SUBLANES = 8
DIL_RES_PER_STEP = 4
SB_Q = 256
SB_K = 256
SB_HEADS_PER_STEP = 4
LOG2_E = 1.4426950408889634
SB_DEAD_LOG2 = -151.0
ROUTER_LANE0 = N_EXPERT_GROUPS
EXPERT_TILE = 256
VMEM_LIMIT = 56 * 1024 * 1024
NEG_BIG = -1e30

f32 = jnp.float32
bf16 = jnp.bfloat16


def _rotate(t, cos_f, sin_f, lane):
    swapped = jnp.where(lane < ROPE_DIM // 2,
                        pltpu.roll(t, LANES - ROPE_DIM // 2, 1),
                        pltpu.roll(t, ROPE_DIM // 2, 1))
    return t * cos_f + swapped * sin_f


def _in_proj_kernel(x_ref, w_ref, *rest, dilation, n_rotary_tiles):
    acc = jnp.dot(x_ref[...], w_ref[0].astype(bf16), preferred_element_type=f32)
    if n_rotary_tiles:
        cos_ref, sin_ref, o_ref, *scratch = rest
    else:
        o_ref, *scratch = rest

    def store(slabs):
        if dilation == 1:
            o_ref[0, 0] = jnp.concatenate(slabs, axis=1).astype(o_ref.dtype)
            return
        (acc_ref,) = scratch
        n = acc.shape[0] // dilation
        for s, slab in enumerate(slabs):
            acc_ref[s] = slab
        for r in range(dilation):
            for s in range(len(slabs)):
                o_ref[0, r, :, s * LANES:(s + 1) * LANES] = (
                    acc_ref[s, pl.ds(r, n, stride=dilation), :].astype(o_ref.dtype))

    slabs = [acc[:, s * LANES:(s + 1) * LANES] for s in range(acc.shape[1] // LANES)]
    if n_rotary_tiles:
        cos_f, sin_f = cos_ref[...], sin_ref[...]
        lane = lax.broadcasted_iota(jnp.int32, cos_f.shape, 1)
        is_qk = pl.program_id(1) < n_rotary_tiles
        slabs = [jnp.where(is_qk, _rotate(slab, cos_f, sin_f, lane), slab) for slab in slabs]
    store(slabs)


def _in_proj(x, w_in, layer, col_block, n_out, b, s, dilation, tm, tn, rope=None, n_rotary_tiles=0):
    k = x.shape[1]
    per_b = s // tm
    scratch = [] if dilation == 1 else [pltpu.VMEM((tn // LANES, tm, LANES), f32)]
    tables = list(rope) if n_rotary_tiles else []
    return pl.pallas_call(
        functools.partial(_in_proj_kernel, dilation=dilation, n_rotary_tiles=n_rotary_tiles),
        grid=(b * per_b, n_out // tn),
        in_specs=[pl.BlockSpec((tm, k), lambda i, j: (i, 0)),
                  pl.BlockSpec((1, k, tn), lambda i, j: (layer, 0, col_block(j)))]
                 + [pl.BlockSpec((tm, LANES), lambda i, j: (i, 0))] * len(tables),
        out_specs=pl.BlockSpec((1, dilation, tm // dilation, tn),
                               lambda i, j: (i // per_b, 0, i % per_b, j)),
        out_shape=jax.ShapeDtypeStruct((b, dilation, s // dilation, n_out), bf16),
        scratch_shapes=scratch,
        compiler_params=pltpu.CompilerParams(
            dimension_semantics=("parallel", "parallel"), vmem_limit_bytes=VMEM_LIMIT),
        name=f"in_proj_d{dilation}",
    )(x, w_in, *tables)


def _dil_kernel(q_ref, kc_ref, kp_ref, vc_ref, vp_ref, o_ref, lse_ref, o_scr, lse_scr,
                *, dilation, res_per_step):
    n = pl.program_id(1)
    rr = pl.program_id(2)
    qi = lax.broadcasted_iota(jnp.int32, (BLK, 2 * BLK), 0)
    kj = lax.broadcasted_iota(jnp.int32, (BLK, 2 * BLK), 1)
    first_key = jnp.where(n > 0, 0, BLK)
    valid = (kj >= jnp.maximum(qi, first_key)) & (kj <= qi + BLK)
    scale = HEAD_DIM ** -0.5

    def residue(ri, _):
        rows = pl.ds(rr * res_per_step + ri, BLK, stride=dilation) if dilation > 1 else slice(None)
        heads = range(HEADS_PER_DIL_GROUP)
        sls = [slice(h * HEAD_DIM, (h + 1) * HEAD_DIM) for h in heads]
        ss = [jnp.where(valid,
                        lax.dot_general(q_ref[0, ri, :, sl],
                                        jnp.concatenate([kp_ref[0, ri, :, sl], kc_ref[0, ri, :, sl]], axis=0),
                                        (((1,), (1,)), ((), ())), preferred_element_type=f32) * scale,
                        NEG_BIG)
              for sl in sls]
        ms = [jnp.max(s, axis=-1, keepdims=True) for s in ss]
        ps = [jnp.exp(s - m) for s, m in zip(ss, ms)]
        dens = [jnp.sum(p, axis=-1, keepdims=True) for p in ps]
        os_ = [jnp.dot(p.astype(bf16),
                       jnp.concatenate([vp_ref[0, ri, :, sl], vc_ref[0, ri, :, sl]], axis=0),
                       preferred_element_type=f32) for p, sl in zip(ps, sls)]
        for h in heads:
            o_scr[h, rows, :] = os_[h] / dens[h]
            lse_scr[h, rows, :] = jnp.broadcast_to(ms[h] + jnp.log(dens[h]), (BLK, HEAD_DIM))
        return 0

    lax.fori_loop(0, res_per_step, residue, 0)

    @pl.when(rr == pl.num_programs(2) - 1)
    def _():
        for h in range(HEADS_PER_DIL_GROUP):
            sl = slice(h * HEAD_DIM, (h + 1) * HEAD_DIM)
            o_ref[0, :, sl] = o_scr[h].astype(o_ref.dtype)
            lse_ref[0, :, sl] = lse_scr[h]


def _dilated_attention(qkv, dilation):
    b, _, sub, _ = qkv.shape
    s = sub * dilation
    nb = sub // BLK
    rs = min(dilation, DIL_RES_PER_STEP)

    def cur(c):
        return pl.BlockSpec((1, rs, BLK, GROUP_W), lambda bi, n, rr: (bi, rr, n, c))

    def prev(c):
        return pl.BlockSpec((1, rs, BLK, GROUP_W), lambda bi, n, rr: (bi, rr, jnp.maximum(n - 1, 0), c))

    out_spec = pl.BlockSpec((1, dilation * BLK, GROUP_W), lambda bi, n, rr: (bi, n, 0))
    o, lse = pl.pallas_call(
        functools.partial(_dil_kernel, dilation=dilation, res_per_step=rs),
        grid=(b, nb, dilation // rs),
        in_specs=[cur(0), cur(1), prev(1), cur(2), prev(2)],
        out_specs=[out_spec, out_spec],
        out_shape=[jax.ShapeDtypeStruct((b, s, GROUP_W), bf16),
                   jax.ShapeDtypeStruct((b, s, GROUP_W), f32)],
        scratch_shapes=[pltpu.VMEM((HEADS_PER_DIL_GROUP, dilation * BLK, HEAD_DIM), f32),
                        pltpu.VMEM((HEADS_PER_DIL_GROUP, dilation * BLK, HEAD_DIM), f32)],
        compiler_params=pltpu.CompilerParams(
            dimension_semantics=("parallel", "arbitrary", "arbitrary"), vmem_limit_bytes=VMEM_LIMIT),
        name=f"dilated_attn_d{dilation}",
    )(qkv, qkv, qkv, qkv, qkv)
    return o.reshape(b * s, GROUP_W), lse.reshape(b * s, GROUP_W)


def _sb_kernel(q_ref, k_ref, v_ref, up_ref, o_ref, acc_ref, c_ref):
    i = pl.program_id(2)
    scale2 = HEAD_DIM ** -0.5 * LOG2_E
    row = lax.broadcasted_iota(jnp.int32, (SB_Q, SB_K), 0)
    coli = lax.broadcasted_iota(jnp.int32, (SB_Q, SB_K), 1)
    causal = coli < row
    acc_ref[...] = jnp.zeros_like(acc_ref)
    c_ref[...] = jnp.zeros_like(c_ref)

    def chunk(j, diag):
        start = pl.multiple_of(j * SB_K, SB_K)
        heads = range(SB_HEADS_PER_STEP)
        sls = [slice(hd * HEAD_DIM, (hd + 1) * HEAD_DIM) for hd in heads]
        ts = [lax.dot_general(q_ref[0, :, sl], k_ref[0, pl.ds(start, SB_K), sl],
                              (((1,), (1,)), ((), ())), preferred_element_type=f32) * scale2
              for sl in sls]
        lss, lks, splits = [], [], []
        for t in ts:
            ls = jnp.minimum(t, 0.0) - jnp.log2(1.0 + jnp.exp2(-jnp.abs(t)))
            lk = ls - t
            if diag:
                lk = jnp.where(causal, lk, 0.0)
            hi = lk.astype(bf16)
            lo = (lk - hi.astype(f32)).astype(bf16)
            lss.append(ls)
            lks.append(lk)
            splits.append(jnp.concatenate([hi, lo], axis=1))
        laters = [jnp.dot(x, up_ref[...], preferred_element_type=f32) for x in splits]
        probs = []
        for hd in heads:
            c = c_ref[hd]
            a = jnp.exp2(lss[hd] + laters[hd] + jnp.concatenate([c, c], axis=1))
            if diag:
                a = jnp.where(causal, a, 0.0)
            probs.append(a.astype(bf16))
            c_ref[hd] = c + jnp.sum(lks[hd], axis=1, keepdims=True)
        for hd in heads:
            acc_ref[hd] += jnp.dot(probs[hd], v_ref[0, pl.ds(start, SB_K), sls[hd]],
                                   preferred_element_type=f32)

    chunk(i, True)

    def live():
        return jnp.max(c_ref[...]) > SB_DEAD_LOG2

    def cond(state):
        t, go = state
        return jnp.logical_and(t < i, go)

    def body(state):
        t, _ = state
        chunk(i - 1 - t, False)
        return t + 1, live()

    lax.while_loop(cond, body, (jnp.int32(0), live()))
    for hd in range(SB_HEADS_PER_STEP):
        o_ref[0, :, hd * HEAD_DIM:(hd + 1) * HEAD_DIM] = acc_ref[hd].astype(o_ref.dtype)


def _stick_breaking(main3, col):
    b, s, _ = main3.shape
    w = SB_HEADS_PER_STEP * HEAD_DIM
    qb, kb, vb = col["qb"] // w, col["kb"] // w, col["vb"] // w
    jr = lax.broadcasted_iota(jnp.int32, (2 * SB_K, SB_K), 0)
    sc = lax.broadcasted_iota(jnp.int32, (2 * SB_K, SB_K), 1)
    upper2 = jnp.where((jr % SB_K) > sc, 1.0, 0.0).astype(bf16)
    o = pl.pallas_call(
        _sb_kernel,
        grid=(b, N_SB_HEADS // SB_HEADS_PER_STEP, s // SB_Q),
        in_specs=[pl.BlockSpec((1, SB_Q, w), lambda bi, h, i: (bi, i, qb + h)),
                  pl.BlockSpec((1, s, w), lambda bi, h, i: (bi, 0, kb + h)),
                  pl.BlockSpec((1, s, w), lambda bi, h, i: (bi, 0, vb + h)),
                  pl.BlockSpec((2 * SB_K, SB_K), lambda bi, h, i: (0, 0))],
        out_specs=pl.BlockSpec((1, SB_Q, w), lambda bi, h, i: (bi, i, h)),
        out_shape=jax.ShapeDtypeStruct((b, s, WIDTH_B), bf16),
        scratch_shapes=[pltpu.VMEM((SB_HEADS_PER_STEP, SB_Q, HEAD_DIM), f32),
                        pltpu.VMEM((SB_HEADS_PER_STEP, SB_Q, HEAD_DIM), f32)],
        compiler_params=pltpu.CompilerParams(
            dimension_semantics=("parallel", "parallel", "arbitrary"), vmem_limit_bytes=VMEM_LIMIT),
        name="stick_breaking_attn",
    )(main3, main3, main3, upper2)
    return o.reshape(b * s, WIDTH_B)


def _layer_norm(u, g, b):
    mu = jnp.mean(u, axis=-1, keepdims=True)
    d = u - mu
    var = jnp.mean(d * d, axis=-1, keepdims=True)
    return d * lax.rsqrt(var + LN_EPS) * g + b


def _mix_kernel(o0_ref, o1_ref, o2_ref, l0_ref, l1_ref, l2_ref, ob_ref, ga_ref, gb_ref, h_ref,
                pa_ref, pb_ref, wo_ref, lng_ref, lnb_ref, wr_ref, br_ref,
                h1_ref, route_ref, cnt_ref, carry_ref, *, alpha, h_chunked):
    step = pl.program_id(0)

    @pl.when(step == 0)
    def _():
        carry_ref[...] = jnp.zeros_like(carry_ref)

    l0, l1, l2 = l0_ref[...], l1_ref[...], l2_ref[...]
    m = jnp.maximum(jnp.maximum(l0, l1), l2)
    e0, e1, e2 = jnp.exp(l0 - m), jnp.exp(l1 - m), jnp.exp(l2 - m)
    o_a = (e0 * o0_ref[...].astype(f32) + e1 * o1_ref[...].astype(f32)
           + e2 * o2_ref[...].astype(f32)) / (e0 + e1 + e2)
    ya = jnp.dot(o_a.astype(bf16), pa_ref[...], preferred_element_type=f32)
    yb = jnp.dot(ob_ref[...], pb_ref[...], preferred_element_type=f32)
    merged = (jax.nn.sigmoid(ga_ref[...].astype(f32)) * ya
              + jax.nn.sigmoid(gb_ref[...].astype(f32)) * yb)
    y = jnp.dot(merged.astype(bf16), wo_ref[...], preferred_element_type=f32)
    tm, d_model = y.shape
    n_chunks = d_model // LANES
    if h_chunked:
        h_res = jnp.concatenate(
            [h_ref[pl.ds(c, tm, stride=n_chunks), :] for c in range(n_chunks)], axis=1)
    else:
        h_res = h_ref[...]
    h1 = _layer_norm(alpha * h_res + y, lng_ref[...], lnb_ref[...])
    pitch = h1_ref.shape[0] // tm
    for c in range(n_chunks):
        h1_ref[pl.ds(c, tm, stride=pitch), :] = h1[:, c * LANES:(c + 1) * LANES]
    for c in range(n_chunks + 1, pitch):
        h1_ref[pl.ds(c, tm, stride=pitch), :] = jnp.zeros((tm, LANES), f32)
    h1b = h1.astype(bf16)

    logits = jnp.dot(h1b, wr_ref[...], preferred_element_type=f32) + br_ref[...]
    lane = lax.broadcasted_iota(jnp.int32, (tm, LANES), 1).astype(f32)
    no_lane = float(LANES)
    lc = jnp.where(lane < N_EXPERT_GROUPS, logits, NEG_BIG)
    mc = jnp.max(lc, axis=-1, keepdims=True)
    p_grp = 1.0 / jnp.sum(jnp.exp(lc - mc), axis=-1, keepdims=True)
    g_idx = jnp.min(jnp.where(lc == mc, lane, no_lane), axis=-1, keepdims=True)
    lo_lane = ROUTER_LANE0 + EXPERTS_PER_GROUP * g_idx
    sel = (lane >= lo_lane) & (lane < lo_lane + EXPERTS_PER_GROUP)
    lf = jnp.where(sel, logits, NEG_BIG)
    v1 = jnp.max(lf, axis=-1, keepdims=True)
    i1 = jnp.min(jnp.where(lf == v1, lane, no_lane), axis=-1, keepdims=True)
    lf2 = jnp.where(lane == i1, NEG_BIG, lf)
    v2 = jnp.max(lf2, axis=-1, keepdims=True)
    i2 = jnp.min(jnp.where((lf2 == v2) & sel & (lane != i1), lane, no_lane), axis=-1, keepdims=True)
    t = jnp.exp(v2 - v1)
    w1 = 1.0 / (1.0 + t)
    w2 = t / (1.0 + t)
    e1 = i1 - lo_lane
    e2 = i2 - lo_lane
    a = jnp.minimum(e1, e2)
    b = jnp.maximum(e1, e2)
    pair = a * (2.0 * EXPERTS_PER_GROUP - 1.0 - a) * 0.5 + (b - a - 1.0)
    bucket = PAIRS_PER_GROUP * g_idx + pair
    first_is_a = e1 < e2
    g1 = p_grp * w1
    g2 = p_grp * w2
    w_a = jnp.where(first_is_a, g1, g2)
    w_b = jnp.where(first_is_a, g2, g1)
    h1_ref[pl.ds(n_chunks, tm, stride=pitch), :] = jnp.where(
        lane == 0, w_a, jnp.where(lane == 1, w_b, 0.0))

    hit = lane == bucket
    onehot = jnp.where(hit, 1.0, 0.0)
    r = lax.broadcasted_iota(jnp.int32, (tm, tm), 0)
    c = lax.broadcasted_iota(jnp.int32, (tm, tm), 1)
    lower = jnp.where(c < r, 1.0, 0.0).astype(bf16)
    before = jnp.dot(lower, onehot.astype(bf16), preferred_element_type=f32) + carry_ref[0:1, :]
    rank = jnp.sum(jnp.where(hit, before, 0.0), axis=-1, keepdims=True)
    route = jnp.where(lane == 0, bucket, jnp.where(lane == 1, rank, 0.0))
    route_ref[...] = route.astype(jnp.int32)
    new_carry = carry_ref[0:1, :] + jnp.sum(onehot, axis=0, keepdims=True)
    carry_ref[...] = jnp.broadcast_to(new_carry, carry_ref.shape)
    cnt_ref[...] = jnp.broadcast_to(new_carry, cnt_ref.shape)


def _chunk_pitch(d):
    return -(-(d // LANES + 1) // SUBLANES) * SUBLANES


def _mix(outs, lses, o_b, main2, h, h_chunked, pa, pb, wo, lng, lnb, wr, br, col, alpha, tm):
    t = main2.shape[0]
    d = pa.shape[1]
    pitch = _chunk_pitch(d)
    row = lambda width, c: pl.BlockSpec((tm, width), lambda i: (i, c))
    const = lambda shape: pl.BlockSpec(shape, lambda i: (0, 0), pipeline_mode=pl.Buffered(1))
    h_spec = pl.BlockSpec((tm * (d // LANES), LANES), lambda i: (i, 0)) if h_chunked else row(d, 0)
    in_specs = ([row(GROUP_W, 0)] * 6
                + [row(WIDTH_B, 0), row(d, col["ga"] // d), row(d, col["gb"] // d), h_spec,
                   const(pa.shape), const(pb.shape), const(wo.shape),
                   const(lng.shape), const(lnb.shape), const(wr.shape), const(br.shape)])
    return pl.pallas_call(
        functools.partial(_mix_kernel, alpha=alpha, h_chunked=h_chunked),
        grid=(t // tm,),
        in_specs=in_specs,
        out_specs=[pl.BlockSpec((tm * pitch, LANES), lambda i: (i, 0)), row(LANES, 0),
                   pl.BlockSpec((8, LANES), lambda i: (0, 0))],
        out_shape=[jax.ShapeDtypeStruct((t * pitch, LANES), f32),
                   jax.ShapeDtypeStruct((t, LANES), jnp.int32),
                   jax.ShapeDtypeStruct((8, LANES), f32)],
        scratch_shapes=[pltpu.VMEM((8, LANES), f32)],
        compiler_params=pltpu.CompilerParams(
            dimension_semantics=("arbitrary",), vmem_limit_bytes=VMEM_LIMIT),
        name="mix_ln_router",
    )(*outs, *lses, o_b, main2, main2, h, pa, pb, wo, lng, lnb, wr, br)


def _gather_tokens(idx_ref, n_tok, pitch, src_hbm, dst, sem):
    def body(p, _):
        for k in range(2):
            tok = 2 * p + k
            start = pl.multiple_of(idx_ref[0, 0, tok], SUBLANES)
            pltpu.make_async_copy(src_hbm.at[pl.ds(start, pitch)],
                                  dst.at[pl.ds(pl.multiple_of(tok * pitch, SUBLANES), pitch)],
                                  sem).start(priority=k)
        return 0
    lax.fori_loop(0, n_tok // 2, body, 0, unroll=4)


def _scatter_tokens(idx_ref, n_tok, pitch, src, dst_hbm, sem):
    def body(p, _):
        for k in range(2):
            tok = 2 * p + k
            start = pl.multiple_of(idx_ref[0, 0, tok], SUBLANES)
            pltpu.make_async_copy(src.at[pl.ds(pl.multiple_of(tok * pitch, SUBLANES), pitch)],
                                  dst_hbm.at[pl.ds(start, pitch)], sem).start(priority=k)
        return 0
    lax.fori_loop(0, n_tok // 2, body, 0, unroll=4)


def _expert_kernel(ta_ref, tb_ref, nu_ref, src_cur, src_nxt, dst_cur, x_hbm,
                   wua_ref, wub_ref, wda_ref, wdb_ref, lng_ref, lnb_ref, o_hbm,
                   xbuf, obuf, gsem, ssem, *, alpha, pitch):
    i = pl.program_id(0)
    n = pl.num_programs(0)
    slot = i % 2
    ff = wda_ref.shape[1]
    d = wda_ref.shape[2]
    n_chunks = d // LANES
    x_rows = EXPERT_TILE * pitch
    o_rows = EXPERT_TILE * n_chunks

    @pl.when(i == 0)
    def _():
        _gather_tokens(src_cur, EXPERT_TILE, pitch, x_hbm, xbuf.at[0], gsem.at[0])

    @pl.when(i + 1 < n)
    def _():
        _gather_tokens(src_nxt, EXPERT_TILE, pitch, x_hbm, xbuf.at[1 - slot], gsem.at[1 - slot])

    pltpu.make_async_copy(x_hbm.at[pl.ds(0, x_rows)], xbuf.at[slot], gsem.at[slot]).wait()

    @pl.when(i >= 2)
    def _():
        pltpu.make_async_copy(obuf.at[slot], o_hbm.at[pl.ds(0, o_rows)], ssem.at[slot]).wait()

    @pl.when(i < nu_ref[0])
    def _():
        chunk = lambda c: xbuf[slot, pl.ds(c, EXPERT_TILE, stride=pitch), :]
        x = jnp.concatenate([chunk(c) for c in range(n_chunks)], axis=1)
        gates = chunk(n_chunks)
        xb = x.astype(bf16)

        def expert(wu_ref, wd_ref):
            hh = jnp.dot(xb, wu_ref[0], preferred_element_type=f32)
            a, b = hh[:, :ff], hh[:, ff:]
            act = a * jax.nn.sigmoid(a) * b
            return jnp.dot(act.astype(bf16), wd_ref[0], preferred_element_type=f32)

        y = gates[:, 0:1] * expert(wua_ref, wda_ref) + gates[:, 1:2] * expert(wub_ref, wdb_ref)
        h2 = _layer_norm(alpha * x + y, lng_ref[...], lnb_ref[...])
        for c in range(n_chunks):
            obuf[slot, pl.ds(c, EXPERT_TILE, stride=n_chunks), :] = h2[:, c * LANES:(c + 1) * LANES]

    _scatter_tokens(dst_cur, EXPERT_TILE, n_chunks, obuf.at[slot], o_hbm, ssem.at[slot])

    @pl.when(i == n - 1)
    def _():
        pltpu.make_async_copy(obuf.at[slot], o_hbm.at[pl.ds(0, o_rows)], ssem.at[slot]).wait()
        pltpu.make_async_copy(obuf.at[1 - slot], o_hbm.at[pl.ds(0, o_rows)], ssem.at[1 - slot]).wait()


def _experts(h1x, src, dst, tile_a, tile_b, n_used, wu, wd, layer, lng, lnb, alpha, n_tok_out):
    d = wd.shape[2]
    pitch = _chunk_pitch(d)
    n_chunks = d // LANES
    n_tiles = src.shape[0]
    ff2 = wu.shape[2]
    ff = wd.shape[1]
    e0 = layer * N_EXPERTS
    idx_spec = lambda f: pl.BlockSpec((1, 1, EXPERT_TILE), f, memory_space=pltpu.SMEM)
    const = lambda shape: pl.BlockSpec(shape, lambda i, ta, tb, nu: (0, 0))
    grid_spec = pltpu.PrefetchScalarGridSpec(
        num_scalar_prefetch=3,
        grid=(n_tiles,),
        in_specs=[
            idx_spec(lambda i, ta, tb, nu: (i, 0, 0)),
            idx_spec(lambda i, ta, tb, nu: (jnp.minimum(i + 1, n_tiles - 1), 0, 0)),
            idx_spec(lambda i, ta, tb, nu: (i, 0, 0)),
            pl.BlockSpec(memory_space=pl.ANY),
            pl.BlockSpec((1, d, ff2), lambda i, ta, tb, nu: (e0 + ta[i], 0, 0)),
            pl.BlockSpec((1, d, ff2), lambda i, ta, tb, nu: (e0 + tb[i], 0, 0)),
            pl.BlockSpec((1, ff, d), lambda i, ta, tb, nu: (e0 + ta[i], 0, 0)),
            pl.BlockSpec((1, ff, d), lambda i, ta, tb, nu: (e0 + tb[i], 0, 0)),
            const(lng.shape), const(lnb.shape),
        ],
        out_specs=pl.BlockSpec(memory_space=pl.ANY),
        scratch_shapes=[pltpu.VMEM((2, EXPERT_TILE * pitch, LANES), f32),
                        pltpu.VMEM((2, EXPERT_TILE * n_chunks, LANES), f32),
                        pltpu.SemaphoreType.DMA((2,)), pltpu.SemaphoreType.DMA((2,))],
    )
    return pl.pallas_call(
        functools.partial(_expert_kernel, alpha=alpha, pitch=pitch),
        grid_spec=grid_spec,
        out_shape=jax.ShapeDtypeStruct((n_tok_out * n_chunks, LANES), f32),
        compiler_params=pltpu.CompilerParams(
            dimension_semantics=("arbitrary",), vmem_limit_bytes=VMEM_LIMIT),
        name="experts_ln",
    )(tile_a, tile_b, n_used, src, src, dst, h1x, wu, wu, wd, wd, lng, lnb)


def _unchunk_kernel(h_ref, o_ref):
    tm, d = o_ref.shape
    n_chunks = d // LANES
    for c in range(n_chunks):
        o_ref[:, c * LANES:(c + 1) * LANES] = (
            h_ref[pl.ds(c, tm, stride=n_chunks), :].astype(o_ref.dtype))


def _unchunk(h, t, d, dtype, tm):
    n_chunks = d // LANES
    return pl.pallas_call(
        _unchunk_kernel,
        grid=(t // tm,),
        in_specs=[pl.BlockSpec((tm * n_chunks, LANES), lambda i: (i, 0))],
        out_specs=pl.BlockSpec((tm, d), lambda i: (i, 0)),
        out_shape=jax.ShapeDtypeStruct((t, d), dtype),
        compiler_params=pltpu.CompilerParams(
            dimension_semantics=("parallel",), vmem_limit_bytes=VMEM_LIMIT),
        name="unchunk",
    )(h)


def _rope_tables(positions):
    b, s = positions.shape
    half = ROPE_DIM // 2
    inv_freq = ROPE_THETA ** (-2.0 * jnp.arange(half, dtype=f32) / ROPE_DIM)
    ang = positions.astype(f32).reshape(b * s, 1) * inv_freq
    cos, sin = jnp.cos(ang), jnp.sin(ang)
    pad = LANES - ROPE_DIM
    cos_f = jnp.concatenate([cos, cos, jnp.ones((b * s, pad), f32)], axis=-1)
    sin_f = jnp.concatenate([-sin, sin, jnp.zeros((b * s, pad), f32)], axis=-1)
    return cos_f, sin_f


def _bucket_experts():
    a_ids, b_ids = [], []
    for g in range(N_EXPERT_GROUPS):
        for a in range(EXPERTS_PER_GROUP):
            for b in range(a + 1, EXPERTS_PER_GROUP):
                a_ids.append(g * EXPERTS_PER_GROUP + a)
                b_ids.append(g * EXPERTS_PER_GROUP + b)
    return jnp.array(a_ids, jnp.int32), jnp.array(b_ids, jnp.int32)


def _route_plan(route, cnt, t, n_tiles):
    bucket = route[:, 0]
    rank = route[:, 1]
    counts = cnt[0, :N_BUCKETS].astype(jnp.int32)
    padded = (counts + EXPERT_TILE - 1) // EXPERT_TILE * EXPERT_TILE
    ends = jnp.cumsum(padded)
    offs = ends - padded
    pos = offs[bucket] + rank
    n_used = (ends[-1] // EXPERT_TILE).astype(jnp.int32)
    tile_start = jnp.arange(n_tiles, dtype=jnp.int32) * EXPERT_TILE
    tile_bucket = jnp.sum((tile_start[:, None] >= ends[None, :]).astype(jnp.int32), axis=1)
    tile_bucket = jnp.minimum(tile_bucket, N_BUCKETS - 1)
    last_used = tile_bucket[jnp.maximum(n_used - 1, 0)]
    tile_bucket = jnp.where(jnp.arange(n_tiles) < n_used, tile_bucket, last_used)
    a_ids, b_ids = _bucket_experts()
    n_slots = n_tiles * EXPERT_TILE
    src = jnp.zeros((n_slots,), jnp.int32).at[pos].set(jnp.arange(t, dtype=jnp.int32))
    slot = jnp.arange(n_slots, dtype=jnp.int32)
    slot_bucket = jnp.repeat(tile_bucket, EXPERT_TILE)
    valid = (slot < ends[-1]) & (slot - offs[slot_bucket] < counts[slot_bucket])
    dummy = t + slot - jnp.cumsum(counts)[slot_bucket]
    dst = jnp.where(valid, src, dummy).astype(jnp.int32)
    shape = (n_tiles, 1, EXPERT_TILE)
    return (src.reshape(shape), dst.reshape(shape), a_ids[tile_bucket], b_ids[tile_bucket],
            n_used.reshape(1))


def kernel(x, positions, w_in, p_a, p_b, w_o, ln_g, ln_b, router_coarse, router_coarse_bias,
           router_fine, router_fine_bias, w_up, w_down):
    b, s, d = x.shape
    depth = w_in.shape[0]
    t = b * s
    alpha = float((2 * depth) ** 0.25)
    assert s % (DIL_CONFIGS[-1][1] * BLK) == 0 and s % SB_Q == 0 and d % GROUP_W == 0

    sizes = (("ga", d), ("gb", d), ("qb", WIDTH_B), ("kb", WIDTH_B), ("vb", WIDTH_B))
    col, off = {}, 0
    for name, width in sizes:
        col[name] = off
        off += width
    a3, b3 = 3 * WIDTH_A, 3 * WIDTH_B

    rope = _rope_tables(positions)
    tm_proj = min(2048, s)
    tm_mix = min(256, t)
    n_tiles = t // EXPERT_TILE + N_BUCKETS
    n_chunks = d // LANES
    wu_all = w_up.astype(bf16).reshape((depth * N_EXPERTS,) + w_up.shape[2:])
    wd_all = w_down.astype(bf16).reshape((depth * N_EXPERTS,) + w_down.shape[2:])

    h = x.reshape(t, d)
    hb = h.astype(bf16)
    for l in range(depth):
        gate_blocks = 2 * d // GROUP_W
        main_block = lambda j: jnp.where(j < gate_blocks, (a3 + b3) // GROUP_W + j,
                                         a3 // GROUP_W + j - gate_blocks)
        main = _in_proj(hb, w_in, l, main_block, 2 * d + b3, b, s, 1, tm_proj, GROUP_W)
        main2 = main.reshape(t, main.shape[-1])
        outs, lses = [], []
        for g, (_, dilation) in enumerate(DIL_CONFIGS):
            group_block = lambda j, g=g: j * N_DIL_GROUPS + g
            qkv = _in_proj(hb, w_in, l, group_block, 3 * GROUP_W, b, s, dilation, tm_proj, GROUP_W,
                           rope=rope, n_rotary_tiles=2)
            o, lse = _dilated_attention(qkv, dilation)
            outs.append(o)
            lses.append(lse)
        o_b = _stick_breaking(main.reshape(b, s, main.shape[-1]), col)

        wr = jnp.zeros((d, LANES), f32)
        wr = wr.at[:, :N_EXPERT_GROUPS].set(router_coarse[l])
        wr = wr.at[:, ROUTER_LANE0:ROUTER_LANE0 + N_EXPERTS].set(router_fine[l]).astype(bf16)
        br = jnp.zeros((1, LANES), f32)
        br = br.at[0, :N_EXPERT_GROUPS].set(router_coarse_bias[l])
        br = br.at[0, ROUTER_LANE0:ROUTER_LANE0 + N_EXPERTS].set(router_fine_bias[l])
        h1x, route, cnt = _mix(
            outs, lses, o_b, main2, h, l > 0, p_a[l].astype(bf16), p_b[l].astype(bf16),
            w_o[l].astype(bf16), ln_g[l, 0].reshape(1, d), ln_b[l, 0].reshape(1, d), wr, br, col,
            alpha, tm_mix)

        src, dst, tile_a, tile_b, n_used = _route_plan(route, cnt, t, n_tiles)
        h = _experts(h1x, src * _chunk_pitch(d), dst * n_chunks, tile_a, tile_b, n_used, wu_all, wd_all,
                     l, ln_g[l, 1].reshape(1, d), ln_b[l, 1].reshape(1, d), alpha,
                     n_tiles * EXPERT_TILE)
        if l + 1 < depth:
            hb = _unchunk(h, t, d, bf16, min(512, t))
    return _unchunk(h, t, d, f32, min(512, t)).reshape(b, s, d)
```

```python
import functools

import jax
import jax.numpy as jnp
from jax import lax
from jax.experimental import pallas as pl
from jax.experimental.pallas import tpu as pltpu

HEAD_DIM = 128
DIL_CONFIGS = ((128, 1), (512, 4), (2048, 16))
N_DIL_GROUPS = 3
HEADS_PER_DIL_GROUP = 4
N_SB_HEADS = 8
ROPE_THETA = 500000.0
ROPE_DIM = HEAD_DIM // 4
N_EXPERT_GROUPS = 4
EXPERTS_PER_GROUP = 4
N_EXPERTS = N_EXPERT_GROUPS * EXPERTS_PER_GROUP
PAIRS_PER_GROUP = EXPERTS_PER_GROUP * (EXPERTS_PER_GROUP - 1) // 2
N_BUCKETS = N_EXPERT_GROUPS * PAIRS_PER_GROUP
LN_EPS = 1e-5

WIDTH_A = N_DIL_GROUPS * HEADS_PER_DIL_GROUP * HEAD_DIM
GROUP_W = HEADS_PER_DIL_GROUP * HEAD_DIM
WIDTH_B = N_SB_HEADS * HEAD_DIM

BLK = 128
LANES = 128
SUBLANES = 8
DIL_RES_PER_STEP = 4
SB_Q = 256
SB_K = 256
SB_HEADS_PER_STEP = 4
LOG2_E = 1.4426950408889634
SB_DEAD_LOG2 = -151.0
ROUTER_LANE0 = N_EXPERT_GROUPS
EXPERT_TILE = 256
IN_PROJ_ROW_CHUNKS = 8
VMEM_LIMIT = 56 * 1024 * 1024
NEG_BIG = -1e30

f32 = jnp.float32
bf16 = jnp.bfloat16


def _rotate(t, cos_f, sin_f, lane):
    swapped = jnp.where(lane < ROPE_DIM // 2,
                        pltpu.roll(t, LANES - ROPE_DIM // 2, 1),
                        pltpu.roll(t, ROPE_DIM // 2, 1))
    return t * cos_f + swapped * sin_f


def _in_proj_kernel(x_ref, w_ref, *rest, dilation, n_rotary_tiles):
    if n_rotary_tiles:
        cos_ref, sin_ref, o_ref, *scratch = rest
    else:
        o_ref, *scratch = rest
    tm = x_ref.shape[0]
    w = w_ref[0].astype(bf16)
    if dilation == 1 and not n_rotary_tiles:
        o_ref[0, 0] = jnp.dot(x_ref[...], w, preferred_element_type=f32).astype(o_ref.dtype)
        return

    n_chunks = IN_PROJ_ROW_CHUNKS if tm % (IN_PROJ_ROW_CHUNKS * dilation * SUBLANES) == 0 else 1
    rows = tm // n_chunks
    n_slabs = w.shape[1] // LANES

    def matmul(c):
        return jnp.dot(x_ref[c * rows:(c + 1) * rows, :], w, preferred_element_type=f32)

    def epilogue(c, acc):
        slabs = [acc[:, s * LANES:(s + 1) * LANES] for s in range(n_slabs)]
        if n_rotary_tiles:
            cos_f = cos_ref[c * rows:(c + 1) * rows, :]
            sin_f = sin_ref[c * rows:(c + 1) * rows, :]
            lane = lax.broadcasted_iota(jnp.int32, cos_f.shape, 1)
            is_qk = pl.program_id(1) < n_rotary_tiles
            slabs = [jnp.where(is_qk, _rotate(slab, cos_f, sin_f, lane), slab) for slab in slabs]
        if dilation == 1:
            o_ref[0, 0, c * rows:(c + 1) * rows, :] = jnp.concatenate(slabs, axis=1).astype(o_ref.dtype)
            return
        (acc_ref,) = scratch
        n = rows // dilation
        for s, slab in enumerate(slabs):
            acc_ref[s, c * rows:(c + 1) * rows, :] = slab
        for r in range(dilation):
            for s in range(n_slabs):
                o_ref[0, r, c * n:(c + 1) * n, s * LANES:(s + 1) * LANES] = (
                    acc_ref[s, pl.ds(c * rows + r, n, stride=dilation), :].astype(o_ref.dtype))

    acc = matmul(0)
    for c in range(n_chunks):
        nxt = matmul(c + 1) if c + 1 < n_chunks else None
        epilogue(c, acc)
        acc = nxt


def _in_proj(x, w_in, layer, col_block, n_out, b, s, dilation, tm, tn, rope=None, n_rotary_tiles=0):
    k = x.shape[1]
    per_b = s // tm
    scratch = [] if dilation == 1 else [pltpu.VMEM((tn // LANES, tm, LANES), f32)]
    tables = list(rope) if n_rotary_tiles else []
    return pl.pallas_call(
        functools.partial(_in_proj_kernel, dilation=dilation, n_rotary_tiles=n_rotary_tiles),
        grid=(b * per_b, n_out // tn),
        in_specs=[pl.BlockSpec((tm, k), lambda i, j: (i, 0)),
                  pl.BlockSpec((1, k, tn), lambda i, j: (layer, 0, col_block(j)))]
                 + [pl.BlockSpec((tm, LANES), lambda i, j: (i, 0))] * len(tables),
        out_specs=pl.BlockSpec((1, dilation, tm // dilation, tn),
                               lambda i, j: (i // per_b, 0, i % per_b, j)),
        out_shape=jax.ShapeDtypeStruct((b, dilation, s // dilation, n_out), bf16),
        scratch_shapes=scratch,
        compiler_params=pltpu.CompilerParams(
            dimension_semantics=("parallel", "parallel"), vmem_limit_bytes=VMEM_LIMIT),
        name=f"in_proj_d{dilation}",
    )(x, w_in, *tables)


def _dil_kernel(q_ref, kc_ref, kp_ref, vc_ref, vp_ref, o_ref, lse_ref, o_scr, lse_scr,
                *, dilation, res_per_step):
    n = pl.program_id(1)
    rr = pl.program_id(2)
    qi = lax.broadcasted_iota(jnp.int32, (BLK, 2 * BLK), 0)
    kj = lax.broadcasted_iota(jnp.int32, (BLK, 2 * BLK), 1)
    first_key = jnp.where(n > 0, 0, BLK)
    valid = (kj >= jnp.maximum(qi, first_key)) & (kj <= qi + BLK)
    scale = HEAD_DIM ** -0.5

    def residue(ri, _):
        rows = pl.ds(rr * res_per_step + ri, BLK, stride=dilation) if dilation > 1 else slice(None)
        heads = range(HEADS_PER_DIL_GROUP)
        sls = [slice(h * HEAD_DIM, (h + 1) * HEAD_DIM) for h in heads]
        ss = [jnp.where(valid,
                        lax.dot_general(q_ref[0, ri, :, sl],
                                        jnp.concatenate([kp_ref[0, ri, :, sl], kc_ref[0, ri, :, sl]], axis=0),
                                        (((1,), (1,)), ((), ())), preferred_element_type=f32) * scale,
                        NEG_BIG)
              for sl in sls]
        ms = [jnp.max(s, axis=-1, keepdims=True) for s in ss]
        ps = [jnp.exp(s - m) for s, m in zip(ss, ms)]
        dens = [jnp.sum(p, axis=-1, keepdims=True) for p in ps]
        os_ = [jnp.dot(p.astype(bf16),
                       jnp.concatenate([vp_ref[0, ri, :, sl], vc_ref[0, ri, :, sl]], axis=0),
                       preferred_element_type=f32) for p, sl in zip(ps, sls)]
        for h in heads:
            o_scr[h, rows, :] = os_[h] / dens[h]
            lse_scr[h, rows, :] = jnp.broadcast_to(ms[h] + jnp.log(dens[h]), (BLK, HEAD_DIM))
        return 0

    lax.fori_loop(0, res_per_step, residue, 0)

    @pl.when(rr == pl.num_programs(2) - 1)
    def _():
        for h in range(HEADS_PER_DIL_GROUP):
            sl = slice(h * HEAD_DIM, (h + 1) * HEAD_DIM)
            o_ref[0, :, sl] = o_scr[h].astype(o_ref.dtype)
            lse_ref[0, :, sl] = lse_scr[h]


def _dilated_attention(qkv, dilation):
    b, _, sub, _ = qkv.shape
    s = sub * dilation
    nb = sub // BLK
    rs = min(dilation, DIL_RES_PER_STEP)

    def cur(c):
        return pl.BlockSpec((1, rs, BLK, GROUP_W), lambda bi, n, rr: (bi, rr, n, c))

    def prev(c):
        return pl.BlockSpec((1, rs, BLK, GROUP_W), lambda bi, n, rr: (bi, rr, jnp.maximum(n - 1, 0), c))

    out_spec = pl.BlockSpec((1, dilation * BLK, GROUP_W), lambda bi, n, rr: (bi, n, 0))
    o, lse = pl.pallas_call(
        functools.partial(_dil_kernel, dilation=dilation, res_per_step=rs),
        grid=(b, nb, dilation // rs),
        in_specs=[cur(0), cur(1), prev(1), cur(2), prev(2)],
        out_specs=[out_spec, out_spec],
        out_shape=[jax.ShapeDtypeStruct((b, s, GROUP_W), bf16),
                   jax.ShapeDtypeStruct((b, s, GROUP_W), f32)],
        scratch_shapes=[pltpu.VMEM((HEADS_PER_DIL_GROUP, dilation * BLK, HEAD_DIM), f32),
                        pltpu.VMEM((HEADS_PER_DIL_GROUP, dilation * BLK, HEAD_DIM), f32)],
        compiler_params=pltpu.CompilerParams(
            dimension_semantics=("parallel", "arbitrary", "arbitrary"), vmem_limit_bytes=VMEM_LIMIT),
        name=f"dilated_attn_d{dilation}",
    )(qkv, qkv, qkv, qkv, qkv)
    return o.reshape(b * s, GROUP_W), lse.reshape(b * s, GROUP_W)


def _sb_kernel(q_ref, k_ref, v_ref, up_ref, o_ref, acc_ref, c_ref):
    i = pl.program_id(2)
    scale2 = HEAD_DIM ** -0.5 * LOG2_E
    row = lax.broadcasted_iota(jnp.int32, (SB_Q, SB_K), 0)
    coli = lax.broadcasted_iota(jnp.int32, (SB_Q, SB_K), 1)
    causal = coli < row
    acc_ref[...] = jnp.zeros_like(acc_ref)
    c_ref[...] = jnp.zeros_like(c_ref)

    def chunk(j, diag):
        start = pl.multiple_of(j * SB_K, SB_K)
        heads = range(SB_HEADS_PER_STEP)
        sls = [slice(hd * HEAD_DIM, (hd + 1) * HEAD_DIM) for hd in heads]
        ts = [lax.dot_general(q_ref[0, :, sl], k_ref[0, pl.ds(start, SB_K), sl],
                              (((1,), (1,)), ((), ())), preferred_element_type=f32) * scale2
              for sl in sls]
        lss, lks, splits = [], [], []
        for t in ts:
            ls = jnp.minimum(t, 0.0) - jnp.log2(1.0 + jnp.exp2(-jnp.abs(t)))
            lk = ls - t
            if diag:
                lk = jnp.where(causal, lk, 0.0)
            hi = lk.astype(bf16)
            lo = (lk - hi.astype(f32)).astype(bf16)
            lss.append(ls)
            lks.append(lk)
            splits.append(jnp.concatenate([hi, lo], axis=1))
        laters = [jnp.dot(x, up_ref[...], preferred_element_type=f32) for x in splits]
        probs = []
        for hd in heads:
            c = c_ref[hd]
            a = jnp.exp2(lss[hd] + laters[hd] + jnp.concatenate([c, c], axis=1))
            if diag:
                a = jnp.where(causal, a, 0.0)
            probs.append(a.astype(bf16))
            c_ref[hd] = c + jnp.sum(lks[hd], axis=1, keepdims=True)
        for hd in heads:
            acc_ref[hd] += jnp.dot(probs[hd], v_ref[0, pl.ds(start, SB_K), sls[hd]],
                                   preferred_element_type=f32)

    chunk(i, True)

    def live():
        return jnp.max(c_ref[...]) > SB_DEAD_LOG2

    def cond(state):
        t, go = state
        return jnp.logical_and(t < i, go)

    def body(state):
        t, _ = state
        chunk(i - 1 - t, False)
        return t + 1, live()

    lax.while_loop(cond, body, (jnp.int32(0), live()))
    for hd in range(SB_HEADS_PER_STEP):
        o_ref[0, :, hd * HEAD_DIM:(hd + 1) * HEAD_DIM] = acc_ref[hd].astype(o_ref.dtype)


def _stick_breaking(main3, col):
    b, s, _ = main3.shape
    w = SB_HEADS_PER_STEP * HEAD_DIM
    qb, kb, vb = col["qb"] // w, col["kb"] // w, col["vb"] // w
    jr = lax.broadcasted_iota(jnp.int32, (2 * SB_K, SB_K), 0)
    sc = lax.broadcasted_iota(jnp.int32, (2 * SB_K, SB_K), 1)
    upper2 = jnp.where((jr % SB_K) > sc, 1.0, 0.0).astype(bf16)
    o = pl.pallas_call(
        _sb_kernel,
        grid=(b, N_SB_HEADS // SB_HEADS_PER_STEP, s // SB_Q),
        in_specs=[pl.BlockSpec((1, SB_Q, w), lambda bi, h, i: (bi, i, qb + h)),
                  pl.BlockSpec((1, s, w), lambda bi, h, i: (bi, 0, kb + h)),
                  pl.BlockSpec((1, s, w), lambda bi, h, i: (bi, 0, vb + h)),
                  pl.BlockSpec((2 * SB_K, SB_K), lambda bi, h, i: (0, 0))],
        out_specs=pl.BlockSpec((1, SB_Q, w), lambda bi, h, i: (bi, i, h)),
        out_shape=jax.ShapeDtypeStruct((b, s, WIDTH_B), bf16),
        scratch_shapes=[pltpu.VMEM((SB_HEADS_PER_STEP, SB_Q, HEAD_DIM), f32),
                        pltpu.VMEM((SB_HEADS_PER_STEP, SB_Q, HEAD_DIM), f32)],
        compiler_params=pltpu.CompilerParams(
            dimension_semantics=("parallel", "parallel", "arbitrary"), vmem_limit_bytes=VMEM_LIMIT),
        name="stick_breaking_attn",
    )(main3, main3, main3, upper2)
    return o.reshape(b * s, WIDTH_B)


def _layer_norm(u, g, b):
    mu = jnp.mean(u, axis=-1, keepdims=True)
    d = u - mu
    var = jnp.mean(d * d, axis=-1, keepdims=True)
    return d * lax.rsqrt(var + LN_EPS) * g + b


def _mix_kernel(o0_ref, o1_ref, o2_ref, l0_ref, l1_ref, l2_ref, ob_ref, ga_ref, gb_ref, h_ref,
                pa_ref, pb_ref, wo_ref, lng_ref, lnb_ref, wr_ref, br_ref,
                h1_ref, route_ref, cnt_ref, carry_ref, *, alpha, h_chunked):
    step = pl.program_id(0)

    @pl.when(step == 0)
    def _():
        carry_ref[...] = jnp.zeros_like(carry_ref)

    l0, l1, l2 = l0_ref[...], l1_ref[...], l2_ref[...]
    m = jnp.maximum(jnp.maximum(l0, l1), l2)
    e0, e1, e2 = jnp.exp(l0 - m), jnp.exp(l1 - m), jnp.exp(l2 - m)
    o_a = (e0 * o0_ref[...].astype(f32) + e1 * o1_ref[...].astype(f32)
           + e2 * o2_ref[...].astype(f32)) / (e0 + e1 + e2)
    ya = jnp.dot(o_a.astype(bf16), pa_ref[...], preferred_element_type=f32)
    yb = jnp.dot(ob_ref[...], pb_ref[...], preferred_element_type=f32)
    merged = (jax.nn.sigmoid(ga_ref[...].astype(f32)) * ya
              + jax.nn.sigmoid(gb_ref[...].astype(f32)) * yb)
    y = jnp.dot(merged.astype(bf16), wo_ref[...], preferred_element_type=f32)
    tm, d_model = y.shape
    n_chunks = d_model // LANES
    if h_chunked:
        h_res = jnp.concatenate(
            [h_ref[pl.ds(c, tm, stride=n_chunks), :] for c in range(n_chunks)], axis=1)
    else:
        h_res = h_ref[...]
    h1 = _layer_norm(alpha * h_res + y, lng_ref[...], lnb_ref[...])
    pitch = h1_ref.shape[0] // tm
    for c in range(n_chunks):
        h1_ref[pl.ds(c, tm, stride=pitch), :] = h1[:, c * LANES:(c + 1) * LANES]
    for c in range(n_chunks + 1, pitch):
        h1_ref[pl.ds(c, tm, stride=pitch), :] = jnp.zeros((tm, LANES), f32)
    h1b = h1.astype(bf16)

    logits = jnp.dot(h1b, wr_ref[...], preferred_element_type=f32) + br_ref[...]
    lane = lax.broadcasted_iota(jnp.int32, (tm, LANES), 1).astype(f32)
    no_lane = float(LANES)
    lc = jnp.where(lane < N_EXPERT_GROUPS, logits, NEG_BIG)
    mc = jnp.max(lc, axis=-1, keepdims=True)
    p_grp = 1.0 / jnp.sum(jnp.exp(lc - mc), axis=-1, keepdims=True)
    g_idx = jnp.min(jnp.where(lc == mc, lane, no_lane), axis=-1, keepdims=True)
    lo_lane = ROUTER_LANE0 + EXPERTS_PER_GROUP * g_idx
    sel = (lane >= lo_lane) & (lane < lo_lane + EXPERTS_PER_GROUP)
    lf = jnp.where(sel, logits, NEG_BIG)
    v1 = jnp.max(lf, axis=-1, keepdims=True)
    i1 = jnp.min(jnp.where(lf == v1, lane, no_lane), axis=-1, keepdims=True)
    lf2 = jnp.where(lane == i1, NEG_BIG, lf)
    v2 = jnp.max(lf2, axis=-1, keepdims=True)
    i2 = jnp.min(jnp.where((lf2 == v2) & sel & (lane != i1), lane, no_lane), axis=-1, keepdims=True)
    t = jnp.exp(v2 - v1)
    w1 = 1.0 / (1.0 + t)
    w2 = t / (1.0 + t)
    e1 = i1 - lo_lane
    e2 = i2 - lo_lane
    a = jnp.minimum(e1, e2)
    b = jnp.maximum(e1, e2)
    pair = a * (2.0 * EXPERTS_PER_GROUP - 1.0 - a) * 0.5 + (b - a - 1.0)
    bucket = PAIRS_PER_GROUP * g_idx + pair
    first_is_a = e1 < e2
    g1 = p_grp * w1
    g2 = p_grp * w2
    w_a = jnp.where(first_is_a, g1, g2)
    w_b = jnp.where(first_is_a, g2, g1)
    h1_ref[pl.ds(n_chunks, tm, stride=pitch), :] = jnp.where(
        lane == 0, w_a, jnp.where(lane == 1, w_b, 0.0))

    hit = lane == bucket
    onehot = jnp.where(hit, 1.0, 0.0)
    r = lax.broadcasted_iota(jnp.int32, (tm, tm), 0)
    c = lax.broadcasted_iota(jnp.int32, (tm, tm), 1)
    lower = jnp.where(c < r, 1.0, 0.0).astype(bf16)
    before = jnp.dot(lower, onehot.astype(bf16), preferred_element_type=f32) + carry_ref[0:1, :]
    rank = jnp.sum(jnp.where(hit, before, 0.0), axis=-1, keepdims=True)
    route = jnp.where(lane == 0, bucket, jnp.where(lane == 1, rank, 0.0))
    route_ref[...] = route.astype(jnp.int32)
    new_carry = carry_ref[0:1, :] + jnp.sum(onehot, axis=0, keepdims=True)
    carry_ref[...] = jnp.broadcast_to(new_carry, carry_ref.shape)
    cnt_ref[...] = jnp.broadcast_to(new_carry, cnt_ref.shape)


def _chunk_pitch(d):
    return -(-(d // LANES + 1) // SUBLANES) * SUBLANES


def _mix(outs, lses, o_b, main2, h, h_chunked, pa, pb, wo, lng, lnb, wr, br, col, alpha, tm):
    t = main2.shape[0]
    d = pa.shape[1]
    pitch = _chunk_pitch(d)
    row = lambda width, c: pl.BlockSpec((tm, width), lambda i: (i, c))
    const = lambda shape: pl.BlockSpec(shape, lambda i: (0, 0), pipeline_mode=pl.Buffered(1))
    h_spec = pl.BlockSpec((tm * (d // LANES), LANES), lambda i: (i, 0)) if h_chunked else row(d, 0)
    in_specs = ([row(GROUP_W, 0)] * 6
                + [row(WIDTH_B, 0), row(d, col["ga"] // d), row(d, col["gb"] // d), h_spec,
                   const(pa.shape), const(pb.shape), const(wo.shape),
                   const(lng.shape), const(lnb.shape), const(wr.shape), const(br.shape)])
    return pl.pallas_call(
        functools.partial(_mix_kernel, alpha=alpha, h_chunked=h_chunked),
        grid=(t // tm,),
        in_specs=in_specs,
        out_specs=[pl.BlockSpec((tm * pitch, LANES), lambda i: (i, 0)), row(LANES, 0),
                   pl.BlockSpec((8, LANES), lambda i: (0, 0))],
        out_shape=[jax.ShapeDtypeStruct((t * pitch, LANES), f32),
                   jax.ShapeDtypeStruct((t, LANES), jnp.int32),
                   jax.ShapeDtypeStruct((8, LANES), f32)],
        scratch_shapes=[pltpu.VMEM((8, LANES), f32)],
        compiler_params=pltpu.CompilerParams(
            dimension_semantics=("arbitrary",), vmem_limit_bytes=VMEM_LIMIT),
        name="mix_ln_router",
    )(*outs, *lses, o_b, main2, main2, h, pa, pb, wo, lng, lnb, wr, br)


def _gather_tokens(idx_ref, n_tok, pitch, src_hbm, dst, sem):
    def body(p, _):
        for k in range(2):
            tok = 2 * p + k
            start = pl.multiple_of(idx_ref[0, 0, tok], SUBLANES)
            pltpu.make_async_copy(src_hbm.at[pl.ds(start, pitch)],
                                  dst.at[pl.ds(pl.multiple_of(tok * pitch, SUBLANES), pitch)],
                                  sem).start(priority=k)
        return 0
    lax.fori_loop(0, n_tok // 2, body, 0, unroll=4)


def _scatter_tokens(idx_ref, n_tok, pitch, src, dst_hbm, sem):
    def body(p, _):
        for k in range(2):
            tok = 2 * p + k
            start = pl.multiple_of(idx_ref[0, 0, tok], SUBLANES)
            pltpu.make_async_copy(src.at[pl.ds(pl.multiple_of(tok * pitch, SUBLANES), pitch)],
                                  dst_hbm.at[pl.ds(start, pitch)], sem).start(priority=k)
        return 0
    lax.fori_loop(0, n_tok // 2, body, 0, unroll=4)


def _expert_kernel(ta_ref, tb_ref, nu_ref, src_cur, src_nxt, dst_cur, x_hbm,
                   wua_ref, wub_ref, wda_ref, wdb_ref, lng_ref, lnb_ref, o_hbm,
                   xbuf, obuf, gsem, ssem, *, alpha, pitch):
    i = pl.program_id(0)
    n = pl.num_programs(0)
    slot = i % 2
    ff = wda_ref.shape[1]
    d = wda_ref.shape[2]
    n_chunks = d // LANES
    x_rows = EXPERT_TILE * pitch
    o_rows = EXPERT_TILE * n_chunks

    @pl.when(i == 0)
    def _():
        _gather_tokens(src_cur, EXPERT_TILE, pitch, x_hbm, xbuf.at[0], gsem.at[0])

    @pl.when(i + 1 < n)
    def _():
        _gather_tokens(src_nxt, EXPERT_TILE, pitch, x_hbm, xbuf.at[1 - slot], gsem.at[1 - slot])

    pltpu.make_async_copy(x_hbm.at[pl.ds(0, x_rows)], xbuf.at[slot], gsem.at[slot]).wait()

    @pl.when(i >= 2)
    def _():
        pltpu.make_async_copy(obuf.at[slot], o_hbm.at[pl.ds(0, o_rows)], ssem.at[slot]).wait()

    @pl.when(i < nu_ref[0])
    def _():
        chunk = lambda c: xbuf[slot, pl.ds(c, EXPERT_TILE, stride=pitch), :]
        x = jnp.concatenate([chunk(c) for c in range(n_chunks)], axis=1)
        gates = chunk(n_chunks)
        xb = x.astype(bf16)

        ups = [jnp.dot(xb, wu_ref[0], preferred_element_type=f32) for wu_ref in (wua_ref, wub_ref)]
        acts = [(hh[:, :ff] * jax.nn.sigmoid(hh[:, :ff]) * hh[:, ff:]).astype(bf16) for hh in ups]
        y_a, y_b = [jnp.dot(act, wd_ref[0], preferred_element_type=f32)
                    for act, wd_ref in zip(acts, (wda_ref, wdb_ref))]
        y = gates[:, 0:1] * y_a + gates[:, 1:2] * y_b
        h2 = _layer_norm(alpha * x + y, lng_ref[...], lnb_ref[...])
        for c in range(n_chunks):
            obuf[slot, pl.ds(c, EXPERT_TILE, stride=n_chunks), :] = h2[:, c * LANES:(c + 1) * LANES]

    _scatter_tokens(dst_cur, EXPERT_TILE, n_chunks, obuf.at[slot], o_hbm, ssem.at[slot])

    @pl.when(i == n - 1)
    def _():
        pltpu.make_async_copy(obuf.at[slot], o_hbm.at[pl.ds(0, o_rows)], ssem.at[slot]).wait()
        pltpu.make_async_copy(obuf.at[1 - slot], o_hbm.at[pl.ds(0, o_rows)], ssem.at[1 - slot]).wait()


def _experts(h1x, src, dst, tile_a, tile_b, n_used, wu, wd, layer, lng, lnb, alpha, n_tok_out):
    d = wd.shape[2]
    pitch = _chunk_pitch(d)
    n_chunks = d // LANES
    n_tiles = src.shape[0]
    ff2 = wu.shape[2]
    ff = wd.shape[1]
    e0 = layer * N_EXPERTS
    idx_spec = lambda f: pl.BlockSpec((1, 1, EXPERT_TILE), f, memory_space=pltpu.SMEM)
    const = lambda shape: pl.BlockSpec(shape, lambda i, ta, tb, nu: (0, 0))
    grid_spec = pltpu.PrefetchScalarGridSpec(
        num_scalar_prefetch=3,
        grid=(n_tiles,),
        in_specs=[
            idx_spec(lambda i, ta, tb, nu: (i, 0, 0)),
            idx_spec(lambda i, ta, tb, nu: (jnp.minimum(i + 1, n_tiles - 1), 0, 0)),
            idx_spec(lambda i, ta, tb, nu: (i, 0, 0)),
            pl.BlockSpec(memory_space=pl.ANY),
            pl.BlockSpec((1, d, ff2), lambda i, ta, tb, nu: (e0 + ta[i], 0, 0)),
            pl.BlockSpec((1, d, ff2), lambda i, ta, tb, nu: (e0 + tb[i], 0, 0)),
            pl.BlockSpec((1, ff, d), lambda i, ta, tb, nu: (e0 + ta[i], 0, 0)),
            pl.BlockSpec((1, ff, d), lambda i, ta, tb, nu: (e0 + tb[i], 0, 0)),
            const(lng.shape), const(lnb.shape),
        ],
        out_specs=pl.BlockSpec(memory_space=pl.ANY),
        scratch_shapes=[pltpu.VMEM((2, EXPERT_TILE * pitch, LANES), f32),
                        pltpu.VMEM((2, EXPERT_TILE * n_chunks, LANES), f32),
                        pltpu.SemaphoreType.DMA((2,)), pltpu.SemaphoreType.DMA((2,))],
    )
    return pl.pallas_call(
        functools.partial(_expert_kernel, alpha=alpha, pitch=pitch),
        grid_spec=grid_spec,
        out_shape=jax.ShapeDtypeStruct((n_tok_out * n_chunks, LANES), f32),
        compiler_params=pltpu.CompilerParams(
            dimension_semantics=("arbitrary",), vmem_limit_bytes=VMEM_LIMIT),
        name="experts_ln",
    )(tile_a, tile_b, n_used, src, src, dst, h1x, wu, wu, wd, wd, lng, lnb)


def _unchunk_kernel(h_ref, o_ref):
    tm, d = o_ref.shape
    n_chunks = d // LANES
    for c in range(n_chunks):
        o_ref[:, c * LANES:(c + 1) * LANES] = (
            h_ref[pl.ds(c, tm, stride=n_chunks), :].astype(o_ref.dtype))


def _unchunk(h, t, d, dtype, tm):
    n_chunks = d // LANES
    return pl.pallas_call(
        _unchunk_kernel,
        grid=(t // tm,),
        in_specs=[pl.BlockSpec((tm * n_chunks, LANES), lambda i: (i, 0))],
        out_specs=pl.BlockSpec((tm, d), lambda i: (i, 0)),
        out_shape=jax.ShapeDtypeStruct((t, d), dtype),
        compiler_params=pltpu.CompilerParams(
            dimension_semantics=("parallel",), vmem_limit_bytes=VMEM_LIMIT),
        name="unchunk",
    )(h)


def _rope_tables(positions):
    b, s = positions.shape
    half = ROPE_DIM // 2
    inv_freq = ROPE_THETA ** (-2.0 * jnp.arange(half, dtype=f32) / ROPE_DIM)
    ang = positions.astype(f32).reshape(b * s, 1) * inv_freq
    cos, sin = jnp.cos(ang), jnp.sin(ang)
    pad = LANES - ROPE_DIM
    cos_f = jnp.concatenate([cos, cos, jnp.ones((b * s, pad), f32)], axis=-1)
    sin_f = jnp.concatenate([-sin, sin, jnp.zeros((b * s, pad), f32)], axis=-1)
    return cos_f, sin_f


def _bucket_experts():
    a_ids, b_ids = [], []
    for g in range(N_EXPERT_GROUPS):
        for a in range(EXPERTS_PER_GROUP):
            for b in range(a + 1, EXPERTS_PER_GROUP):
                a_ids.append(g * EXPERTS_PER_GROUP + a)
                b_ids.append(g * EXPERTS_PER_GROUP + b)
    return jnp.array(a_ids, jnp.int32), jnp.array(b_ids, jnp.int32)


def _route_plan(route, cnt, t, n_tiles):
    bucket = route[:, 0]
    rank = route[:, 1]
    counts = cnt[0, :N_BUCKETS].astype(jnp.int32)
    padded = (counts + EXPERT_TILE - 1) // EXPERT_TILE * EXPERT_TILE
    ends = jnp.cumsum(padded)
    offs = ends - padded
    pos = offs[bucket] + rank
    n_used = (ends[-1] // EXPERT_TILE).astype(jnp.int32)
    tile_start = jnp.arange(n_tiles, dtype=jnp.int32) * EXPERT_TILE
    tile_bucket = jnp.sum((tile_start[:, None] >= ends[None, :]).astype(jnp.int32), axis=1)
    tile_bucket = jnp.minimum(tile_bucket, N_BUCKETS - 1)
    last_used = tile_bucket[jnp.maximum(n_used - 1, 0)]
    tile_bucket = jnp.where(jnp.arange(n_tiles) < n_used, tile_bucket, last_used)
    a_ids, b_ids = _bucket_experts()
    n_slots = n_tiles * EXPERT_TILE
    src = jnp.zeros((n_slots,), jnp.int32).at[pos].set(jnp.arange(t, dtype=jnp.int32))
    slot = jnp.arange(n_slots, dtype=jnp.int32)
    slot_bucket = jnp.repeat(tile_bucket, EXPERT_TILE)
    valid = (slot < ends[-1]) & (slot - offs[slot_bucket] < counts[slot_bucket])
    dummy = t + slot - jnp.cumsum(counts)[slot_bucket]
    dst = jnp.where(valid, src, dummy).astype(jnp.int32)
    shape = (n_tiles, 1, EXPERT_TILE)
    return (src.reshape(shape), dst.reshape(shape), a_ids[tile_bucket], b_ids[tile_bucket],
            n_used.reshape(1))


def kernel(x, positions, w_in, p_a, p_b, w_o, ln_g, ln_b, router_coarse, router_coarse_bias,
           router_fine, router_fine_bias, w_up, w_down):
    b, s, d = x.shape
    depth = w_in.shape[0]
    t = b * s
    alpha = float((2 * depth) ** 0.25)
    assert s % (DIL_CONFIGS[-1][1] * BLK) == 0 and s % SB_Q == 0 and d % GROUP_W == 0

    sizes = (("ga", d), ("gb", d), ("qb", WIDTH_B), ("kb", WIDTH_B), ("vb", WIDTH_B))
    col, off = {}, 0
    for name, width in sizes:
        col[name] = off
        off += width
    a3, b3 = 3 * WIDTH_A, 3 * WIDTH_B

    rope = _rope_tables(positions)
    tm_proj = min(2048, s)
    tm_mix = min(256, t)
    n_tiles = t // EXPERT_TILE + N_BUCKETS
    n_chunks = d // LANES
    wu_all = w_up.astype(bf16).reshape((depth * N_EXPERTS,) + w_up.shape[2:])
    wd_all = w_down.astype(bf16).reshape((depth * N_EXPERTS,) + w_down.shape[2:])

    h = x.reshape(t, d)
    hb = h.astype(bf16)
    for l in range(depth):
        gate_blocks = 2 * d // GROUP_W
        main_block = lambda j: jnp.where(j < gate_blocks, (a3 + b3) // GROUP_W + j,
                                         a3 // GROUP_W + j - gate_blocks)
        main = _in_proj(hb, w_in, l, main_block, 2 * d + b3, b, s, 1, tm_proj, GROUP_W)
        main2 = main.reshape(t, main.shape[-1])
        outs, lses = [], []
        for g, (_, dilation) in enumerate(DIL_CONFIGS):
            group_block = lambda j, g=g: j * N_DIL_GROUPS + g
            qkv = _in_proj(hb, w_in, l, group_block, 3 * GROUP_W, b, s, dilation, tm_proj, GROUP_W,
                           rope=rope, n_rotary_tiles=2)
            o, lse = _dilated_attention(qkv, dilation)
            outs.append(o)
            lses.append(lse)
        o_b = _stick_breaking(main.reshape(b, s, main.shape[-1]), col)

        wr = jnp.zeros((d, LANES), f32)
        wr = wr.at[:, :N_EXPERT_GROUPS].set(router_coarse[l])
        wr = wr.at[:, ROUTER_LANE0:ROUTER_LANE0 + N_EXPERTS].set(router_fine[l]).astype(bf16)
        br = jnp.zeros((1, LANES), f32)
        br = br.at[0, :N_EXPERT_GROUPS].set(router_coarse_bias[l])
        br = br.at[0, ROUTER_LANE0:ROUTER_LANE0 + N_EXPERTS].set(router_fine_bias[l])
        h1x, route, cnt = _mix(
            outs, lses, o_b, main2, h, l > 0, p_a[l].astype(bf16), p_b[l].astype(bf16),
            w_o[l].astype(bf16), ln_g[l, 0].reshape(1, d), ln_b[l, 0].reshape(1, d), wr, br, col,
            alpha, tm_mix)

        src, dst, tile_a, tile_b, n_used = _route_plan(route, cnt, t, n_tiles)
        h = _experts(h1x, src * _chunk_pitch(d), dst * n_chunks, tile_a, tile_b, n_used, wu_all, wd_all,
                     l, ln_g[l, 1].reshape(1, d), ln_b[l, 1].reshape(1, d), alpha,
                     n_tiles * EXPERT_TILE)
        if l + 1 < depth:
            hb = _unchunk(h, t, d, bf16, min(512, t))
    return _unchunk(h, t, d, f32, min(512, t)).reshape(b, s, d)
```

```python
import functools

import jax
import jax.numpy as jnp
from jax import lax
from jax.experimental import pallas as pl
from jax.experimental.pallas import tpu as pltpu

HEAD_DIM = 128
DIL_CONFIGS = ((128, 1), (512, 4), (2048, 16))
N_DIL_GROUPS = 3
HEADS_PER_DIL_GROUP = 4
N_SB_HEADS = 8
ROPE_THETA = 500000.0
ROPE_DIM = HEAD_DIM // 4
N_EXPERT_GROUPS = 4
EXPERTS_PER_GROUP = 4
N_EXPERTS = N_EXPERT_GROUPS * EXPERTS_PER_GROUP
PAIRS_PER_GROUP = EXPERTS_PER_GROUP * (EXPERTS_PER_GROUP - 1) // 2
N_BUCKETS = N_EXPERT_GROUPS * PAIRS_PER_GROUP
LN_EPS = 1e-5

WIDTH_A = N_DIL_GROUPS * HEADS_PER_DIL_GROUP * HEAD_DIM
GROUP_W = HEADS_PER_DIL_GROUP * HEAD_DIM
WIDTH_B = N_SB_HEADS * HEAD_DIM

BLK = 128
LANES = 128
SUBLANES = 8
DIL_RES_PER_STEP = 4
SB_Q = 256
SB_K = 256
SB_HEADS_PER_STEP = 4
LOG2_E = 1.4426950408889634
SB_DEAD_LOG2 = -151.0
ROUTER_LANE0 = N_EXPERT_GROUPS
EXPERT_TILE = 256
IN_PROJ_ROW_CHUNKS = 8
VMEM_LIMIT = 56 * 1024 * 1024
NEG_BIG = -1e30

f32 = jnp.float32
bf16 = jnp.bfloat16


def _rotate(t, cos_f, sin_f, lane):
    swapped = jnp.where(lane < ROPE_DIM // 2,
                        pltpu.roll(t, LANES - ROPE_DIM // 2, 1),
                        pltpu.roll(t, ROPE_DIM // 2, 1))
    return t * cos_f + swapped * sin_f


def _in_proj_kernel(x_ref, w_ref, *rest, dilation, n_rotary_tiles):
    if n_rotary_tiles:
        cos_ref, sin_ref, o_ref, *scratch = rest
    else:
        o_ref, *scratch = rest
    tm = x_ref.shape[0]
    w = w_ref[0].astype(bf16)
    if dilation == 1 and not n_rotary_tiles:
        o_ref[0, 0] = jnp.dot(x_ref[...], w, preferred_element_type=f32).astype(o_ref.dtype)
        return

    n_chunks = IN_PROJ_ROW_CHUNKS if tm % (IN_PROJ_ROW_CHUNKS * dilation * SUBLANES) == 0 else 1
    rows = tm // n_chunks
    n_slabs = w.shape[1] // LANES

    def matmul(c):
        return jnp.dot(x_ref[c * rows:(c + 1) * rows, :], w, preferred_element_type=f32)

    def epilogue(c, acc):
        slabs = [acc[:, s * LANES:(s + 1) * LANES] for s in range(n_slabs)]
        if n_rotary_tiles:
            cos_f = cos_ref[c * rows:(c + 1) * rows, :]
            sin_f = sin_ref[c * rows:(c + 1) * rows, :]
            lane = lax.broadcasted_iota(jnp.int32, cos_f.shape, 1)
            is_qk = pl.program_id(1) < n_rotary_tiles
            slabs = [jnp.where(is_qk, _rotate(slab, cos_f, sin_f, lane), slab) for slab in slabs]
        if dilation == 1:
            o_ref[0, 0, c * rows:(c + 1) * rows, :] = jnp.concatenate(slabs, axis=1).astype(o_ref.dtype)
            return
        (acc_ref,) = scratch
        n = rows // dilation
        for s, slab in enumerate(slabs):
            acc_ref[s, c * rows:(c + 1) * rows, :] = slab
        for r in range(dilation):
            for s in range(n_slabs):
                o_ref[0, r, c * n:(c + 1) * n, s * LANES:(s + 1) * LANES] = (
                    acc_ref[s, pl.ds(c * rows + r, n, stride=dilation), :].astype(o_ref.dtype))

    acc = matmul(0)
    for c in range(n_chunks):
        nxt = matmul(c + 1) if c + 1 < n_chunks else None
        epilogue(c, acc)
        acc = nxt


def _in_proj(x, w_in, layer, col_block, n_out, b, s, dilation, tm, tn, rope=None, n_rotary_tiles=0):
    k = x.shape[1]
    per_b = s // tm
    scratch = [] if dilation == 1 else [pltpu.VMEM((tn // LANES, tm, LANES), f32)]
    tables = list(rope) if n_rotary_tiles else []
    return pl.pallas_call(
        functools.partial(_in_proj_kernel, dilation=dilation, n_rotary_tiles=n_rotary_tiles),
        grid=(b * per_b, n_out // tn),
        in_specs=[pl.BlockSpec((tm, k), lambda i, j: (i, 0)),
                  pl.BlockSpec((1, k, tn), lambda i, j: (layer, 0, col_block(j)))]
                 + [pl.BlockSpec((tm, LANES), lambda i, j: (i, 0))] * len(tables),
        out_specs=pl.BlockSpec((1, dilation, tm // dilation, tn),
                               lambda i, j: (i // per_b, 0, i % per_b, j)),
        out_shape=jax.ShapeDtypeStruct((b, dilation, s // dilation, n_out), bf16),
        scratch_shapes=scratch,
        compiler_params=pltpu.CompilerParams(
            dimension_semantics=("parallel", "parallel"), vmem_limit_bytes=VMEM_LIMIT),
        name=f"in_proj_d{dilation}",
    )(x, w_in, *tables)


def _dil_kernel(q_ref, kc_ref, kp_ref, vc_ref, vp_ref, o_ref, lse_ref, o_scr, lse_scr,
                *, dilation, res_per_step):
    n = pl.program_id(1)
    rr = pl.program_id(2)
    qi = lax.broadcasted_iota(jnp.int32, (BLK, 2 * BLK), 0)
    kj = lax.broadcasted_iota(jnp.int32, (BLK, 2 * BLK), 1)
    first_key = jnp.where(n > 0, 0, BLK)
    valid = (kj >= jnp.maximum(qi, first_key)) & (kj <= qi + BLK)
    scale = HEAD_DIM ** -0.5

    def residue(ri, _):
        rows = pl.ds(rr * res_per_step + ri, BLK, stride=dilation) if dilation > 1 else slice(None)
        heads = range(HEADS_PER_DIL_GROUP)
        sls = [slice(h * HEAD_DIM, (h + 1) * HEAD_DIM) for h in heads]
        ss = [jnp.where(valid,
                        lax.dot_general(q_ref[0, ri, :, sl],
                                        jnp.concatenate([kp_ref[0, ri, :, sl], kc_ref[0, ri, :, sl]], axis=0),
                                        (((1,), (1,)), ((), ())), preferred_element_type=f32) * scale,
                        NEG_BIG)
              for sl in sls]
        ms = [jnp.max(s, axis=-1, keepdims=True) for s in ss]
        ps = [jnp.exp(s - m) for s, m in zip(ss, ms)]
        dens = [jnp.sum(p, axis=-1, keepdims=True) for p in ps]
        os_ = [jnp.dot(p.astype(bf16),
                       jnp.concatenate([vp_ref[0, ri, :, sl], vc_ref[0, ri, :, sl]], axis=0),
                       preferred_element_type=f32) for p, sl in zip(ps, sls)]
        for h in heads:
            o_scr[h, rows, :] = os_[h] / dens[h]
            lse_scr[h, rows, :] = jnp.broadcast_to(ms[h] + jnp.log(dens[h]), (BLK, HEAD_DIM))
        return 0

    lax.fori_loop(0, res_per_step, residue, 0)

    @pl.when(rr == pl.num_programs(2) - 1)
    def _():
        for h in range(HEADS_PER_DIL_GROUP):
            sl = slice(h * HEAD_DIM, (h + 1) * HEAD_DIM)
            o_ref[0, :, sl] = o_scr[h].astype(o_ref.dtype)
            lse_ref[0, :, sl] = lse_scr[h]


def _dilated_attention(qkv, dilation):
    b, _, sub, _ = qkv.shape
    s = sub * dilation
    nb = sub // BLK
    rs = min(dilation, DIL_RES_PER_STEP)

    def cur(c):
        return pl.BlockSpec((1, rs, BLK, GROUP_W), lambda bi, n, rr: (bi, rr, n, c))

    def prev(c):
        return pl.BlockSpec((1, rs, BLK, GROUP_W), lambda bi, n, rr: (bi, rr, jnp.maximum(n - 1, 0), c))

    out_spec = pl.BlockSpec((1, dilation * BLK, GROUP_W), lambda bi, n, rr: (bi, n, 0))
    o, lse = pl.pallas_call(
        functools.partial(_dil_kernel, dilation=dilation, res_per_step=rs),
        grid=(b, nb, dilation // rs),
        in_specs=[cur(0), cur(1), prev(1), cur(2), prev(2)],
        out_specs=[out_spec, out_spec],
        out_shape=[jax.ShapeDtypeStruct((b, s, GROUP_W), bf16),
                   jax.ShapeDtypeStruct((b, s, GROUP_W), f32)],
        scratch_shapes=[pltpu.VMEM((HEADS_PER_DIL_GROUP, dilation * BLK, HEAD_DIM), f32),
                        pltpu.VMEM((HEADS_PER_DIL_GROUP, dilation * BLK, HEAD_DIM), f32)],
        compiler_params=pltpu.CompilerParams(
            dimension_semantics=("parallel", "arbitrary", "arbitrary"), vmem_limit_bytes=VMEM_LIMIT),
        name=f"dilated_attn_d{dilation}",
    )(qkv, qkv, qkv, qkv, qkv)
    return o.reshape(b * s, GROUP_W), lse.reshape(b * s, GROUP_W)


def _sb_kernel(q_ref, k_ref, v_ref, up_ref, o_ref, acc_ref, c_ref):
    i = pl.program_id(2)
    scale2 = HEAD_DIM ** -0.5 * LOG2_E
    row = lax.broadcasted_iota(jnp.int32, (SB_Q, SB_K), 0)
    coli = lax.broadcasted_iota(jnp.int32, (SB_Q, SB_K), 1)
    causal = coli < row
    acc_ref[...] = jnp.zeros_like(acc_ref)
    c_ref[...] = jnp.zeros_like(c_ref)

    def chunk(j, diag):
        start = pl.multiple_of(j * SB_K, SB_K)
        heads = range(SB_HEADS_PER_STEP)
        sls = [slice(hd * HEAD_DIM, (hd + 1) * HEAD_DIM) for hd in heads]
        ts = [lax.dot_general(q_ref[0, :, sl], k_ref[0, pl.ds(start, SB_K), sl],
                              (((1,), (1,)), ((), ())), preferred_element_type=f32) * scale2
              for sl in sls]
        lss, lks, splits = [], [], []
        for t in ts:
            ls = jnp.minimum(t, 0.0) - jnp.log2(1.0 + jnp.exp2(-jnp.abs(t)))
            lk = ls - t
            if diag:
                lk = jnp.where(causal, lk, 0.0)
            hi = lk.astype(bf16)
            lo = (lk - hi.astype(f32)).astype(bf16)
            lss.append(ls)
            lks.append(lk)
            splits.append(jnp.concatenate([hi, lo], axis=1))
        laters = [jnp.dot(x, up_ref[...], preferred_element_type=f32) for x in splits]
        probs = []
        for hd in heads:
            c = c_ref[hd]
            a = jnp.exp2(lss[hd] + laters[hd] + jnp.concatenate([c, c], axis=1))
            if diag:
                a = jnp.where(causal, a, 0.0)
            probs.append(a.astype(bf16))
            c_ref[hd] = c + jnp.sum(lks[hd], axis=1, keepdims=True)
        for hd in heads:
            acc_ref[hd] += jnp.dot(probs[hd], v_ref[0, pl.ds(start, SB_K), sls[hd]],
                                   preferred_element_type=f32)

    chunk(i, True)

    def live():
        return jnp.max(c_ref[...]) > SB_DEAD_LOG2

    def cond(state):
        t, go = state
        return jnp.logical_and(t < i, go)

    def body(state):
        t, _ = state
        chunk(i - 1 - t, False)
        return t + 1, live()

    lax.while_loop(cond, body, (jnp.int32(0), live()))
    for hd in range(SB_HEADS_PER_STEP):
        o_ref[0, :, hd * HEAD_DIM:(hd + 1) * HEAD_DIM] = acc_ref[hd].astype(o_ref.dtype)


def _stick_breaking(main3, col):
    b, s, _ = main3.shape
    w = SB_HEADS_PER_STEP * HEAD_DIM
    qb, kb, vb = col["qb"] // w, col["kb"] // w, col["vb"] // w
    jr = lax.broadcasted_iota(jnp.int32, (2 * SB_K, SB_K), 0)
    sc = lax.broadcasted_iota(jnp.int32, (2 * SB_K, SB_K), 1)
    upper2 = jnp.where((jr % SB_K) > sc, 1.0, 0.0).astype(bf16)
    o = pl.pallas_call(
        _sb_kernel,
        grid=(b, N_SB_HEADS // SB_HEADS_PER_STEP, s // SB_Q),
        in_specs=[pl.BlockSpec((1, SB_Q, w), lambda bi, h, i: (bi, i, qb + h)),
                  pl.BlockSpec((1, s, w), lambda bi, h, i: (bi, 0, kb + h)),
                  pl.BlockSpec((1, s, w), lambda bi, h, i: (bi, 0, vb + h)),
                  pl.BlockSpec((2 * SB_K, SB_K), lambda bi, h, i: (0, 0))],
        out_specs=pl.BlockSpec((1, SB_Q, w), lambda bi, h, i: (bi, i, h)),
        out_shape=jax.ShapeDtypeStruct((b, s, WIDTH_B), bf16),
        scratch_shapes=[pltpu.VMEM((SB_HEADS_PER_STEP, SB_Q, HEAD_DIM), f32),
                        pltpu.VMEM((SB_HEADS_PER_STEP, SB_Q, HEAD_DIM), f32)],
        compiler_params=pltpu.CompilerParams(
            dimension_semantics=("parallel", "parallel", "arbitrary"), vmem_limit_bytes=VMEM_LIMIT),
        name="stick_breaking_attn",
    )(main3, main3, main3, upper2)
    return o.reshape(b * s, WIDTH_B)


def _layer_norm(u, g, b):
    mu = jnp.mean(u, axis=-1, keepdims=True)
    d = u - mu
    var = jnp.mean(d * d, axis=-1, keepdims=True)
    return d * lax.rsqrt(var + LN_EPS) * g + b


def _mix_kernel(o0_ref, o1_ref, o2_ref, l0_ref, l1_ref, l2_ref, ob_ref, ga_ref, gb_ref, h_ref,
                pa_ref, pb_ref, wo_ref, lng_ref, lnb_ref, wr_ref, br_ref,
                h1_ref, route_ref, cnt_ref, carry_ref, *, alpha, h_chunked):
    step = pl.program_id(0)

    @pl.when(step == 0)
    def _():
        carry_ref[...] = jnp.zeros_like(carry_ref)

    l0, l1, l2 = l0_ref[...], l1_ref[...], l2_ref[...]
    m = jnp.maximum(jnp.maximum(l0, l1), l2)
    e0, e1, e2 = jnp.exp(l0 - m), jnp.exp(l1 - m), jnp.exp(l2 - m)
    o_a = (e0 * o0_ref[...].astype(f32) + e1 * o1_ref[...].astype(f32)
           + e2 * o2_ref[...].astype(f32)) / (e0 + e1 + e2)
    ya = jnp.dot(o_a.astype(bf16), pa_ref[...], preferred_element_type=f32)
    yb = jnp.dot(ob_ref[...], pb_ref[...], preferred_element_type=f32)
    merged = (jax.nn.sigmoid(ga_ref[...].astype(f32)) * ya
              + jax.nn.sigmoid(gb_ref[...].astype(f32)) * yb)
    y = jnp.dot(merged.astype(bf16), wo_ref[...], preferred_element_type=f32)
    tm, d_model = y.shape
    n_chunks = d_model // LANES
    if h_chunked:
        h_res = jnp.concatenate(
            [h_ref[pl.ds(c, tm, stride=n_chunks), :] for c in range(n_chunks)], axis=1)
    else:
        h_res = h_ref[...]
    h1 = _layer_norm(alpha * h_res + y, lng_ref[...], lnb_ref[...])
    pitch = h1_ref.shape[0] // tm
    for c in range(n_chunks):
        h1_ref[pl.ds(c, tm, stride=pitch), :] = h1[:, c * LANES:(c + 1) * LANES]
    for c in range(n_chunks + 1, pitch):
        h1_ref[pl.ds(c, tm, stride=pitch), :] = jnp.zeros((tm, LANES), f32)
    h1b = h1.astype(bf16)

    logits = jnp.dot(h1b, wr_ref[...], preferred_element_type=f32) + br_ref[...]
    lane = lax.broadcasted_iota(jnp.int32, (tm, LANES), 1).astype(f32)
    no_lane = float(LANES)
    lc = jnp.where(lane < N_EXPERT_GROUPS, logits, NEG_BIG)
    mc = jnp.max(lc, axis=-1, keepdims=True)
    p_grp = 1.0 / jnp.sum(jnp.exp(lc - mc), axis=-1, keepdims=True)
    g_idx = jnp.min(jnp.where(lc == mc, lane, no_lane), axis=-1, keepdims=True)
    lo_lane = ROUTER_LANE0 + EXPERTS_PER_GROUP * g_idx
    sel = (lane >= lo_lane) & (lane < lo_lane + EXPERTS_PER_GROUP)
    lf = jnp.where(sel, logits, NEG_BIG)
    v1 = jnp.max(lf, axis=-1, keepdims=True)
    i1 = jnp.min(jnp.where(lf == v1, lane, no_lane), axis=-1, keepdims=True)
    lf2 = jnp.where(lane == i1, NEG_BIG, lf)
    v2 = jnp.max(lf2, axis=-1, keepdims=True)
    i2 = jnp.min(jnp.where((lf2 == v2) & sel & (lane != i1), lane, no_lane), axis=-1, keepdims=True)
    t = jnp.exp(v2 - v1)
    w1 = 1.0 / (1.0 + t)
    w2 = t / (1.0 + t)
    e1 = i1 - lo_lane
    e2 = i2 - lo_lane
    a = jnp.minimum(e1, e2)
    b = jnp.maximum(e1, e2)
    pair = a * (2.0 * EXPERTS_PER_GROUP - 1.0 - a) * 0.5 + (b - a - 1.0)
    bucket = PAIRS_PER_GROUP * g_idx + pair
    first_is_a = e1 < e2
    g1 = p_grp * w1
    g2 = p_grp * w2
    w_a = jnp.where(first_is_a, g1, g2)
    w_b = jnp.where(first_is_a, g2, g1)
    h1_ref[pl.ds(n_chunks, tm, stride=pitch), :] = jnp.where(
        lane == 0, w_a, jnp.where(lane == 1, w_b, 0.0))

    hit = lane == bucket
    onehot = jnp.where(hit, 1.0, 0.0)
    r = lax.broadcasted_iota(jnp.int32, (tm, tm), 0)
    c = lax.broadcasted_iota(jnp.int32, (tm, tm), 1)
    lower = jnp.where(c < r, 1.0, 0.0).astype(bf16)
    before = jnp.dot(lower, onehot.astype(bf16), preferred_element_type=f32) + carry_ref[0:1, :]
    rank = jnp.sum(jnp.where(hit, before, 0.0), axis=-1, keepdims=True)
    route = jnp.where(lane == 0, bucket, jnp.where(lane == 1, rank, 0.0))
    route_ref[...] = route.astype(jnp.int32)
    new_carry = carry_ref[0:1, :] + jnp.sum(onehot, axis=0, keepdims=True)
    carry_ref[...] = jnp.broadcast_to(new_carry, carry_ref.shape)
    cnt_ref[...] = jnp.broadcast_to(new_carry, cnt_ref.shape)


def _chunk_pitch(d):
    return -(-(d // LANES + 1) // SUBLANES) * SUBLANES


def _mix(outs, lses, o_b, main2, h, h_chunked, pa, pb, wo, lng, lnb, wr, br, col, alpha, tm):
    t = main2.shape[0]
    d = pa.shape[1]
    pitch = _chunk_pitch(d)
    row = lambda width, c: pl.BlockSpec((tm, width), lambda i: (i, c))
    const = lambda shape: pl.BlockSpec(shape, lambda i: (0, 0), pipeline_mode=pl.Buffered(1))
    h_spec = pl.BlockSpec((tm * (d // LANES), LANES), lambda i: (i, 0)) if h_chunked else row(d, 0)
    in_specs = ([row(GROUP_W, 0)] * 6
                + [row(WIDTH_B, 0), row(d, col["ga"] // d), row(d, col["gb"] // d), h_spec,
                   const(pa.shape), const(pb.shape), const(wo.shape),
                   const(lng.shape), const(lnb.shape), const(wr.shape), const(br.shape)])
    return pl.pallas_call(
        functools.partial(_mix_kernel, alpha=alpha, h_chunked=h_chunked),
        grid=(t // tm,),
        in_specs=in_specs,
        out_specs=[pl.BlockSpec((tm * pitch, LANES), lambda i: (i, 0)), row(LANES, 0),
                   pl.BlockSpec((8, LANES), lambda i: (0, 0))],
        out_shape=[jax.ShapeDtypeStruct((t * pitch, LANES), f32),
                   jax.ShapeDtypeStruct((t, LANES), jnp.int32),
                   jax.ShapeDtypeStruct((8, LANES), f32)],
        scratch_shapes=[pltpu.VMEM((8, LANES), f32)],
        compiler_params=pltpu.CompilerParams(
            dimension_semantics=("arbitrary",), vmem_limit_bytes=VMEM_LIMIT),
        name="mix_ln_router",
    )(*outs, *lses, o_b, main2, main2, h, pa, pb, wo, lng, lnb, wr, br)


def _gather_tokens(idx_ref, n_tok, pitch, src_hbm, dst, sem):
    def body(p, _):
        for k in range(2):
            tok = 2 * p + k
            start = pl.multiple_of(idx_ref[0, 0, tok], SUBLANES)
            pltpu.make_async_copy(src_hbm.at[pl.ds(start, pitch)],
                                  dst.at[pl.ds(pl.multiple_of(tok * pitch, SUBLANES), pitch)],
                                  sem).start(priority=k)
        return 0
    lax.fori_loop(0, n_tok // 2, body, 0, unroll=4)


def _scatter_tokens(idx_ref, n_tok, pitch, src, dst_hbm, sem):
    def body(p, _):
        for k in range(2):
            tok = 2 * p + k
            start = pl.multiple_of(idx_ref[0, 0, tok], SUBLANES)
            pltpu.make_async_copy(src.at[pl.ds(pl.multiple_of(tok * pitch, SUBLANES), pitch)],
                                  dst_hbm.at[pl.ds(start, pitch)], sem).start(priority=k)
        return 0
    lax.fori_loop(0, n_tok // 2, body, 0, unroll=4)


def _expert_kernel(ta_ref, tb_ref, nu_ref, src_cur, src_nxt, dst_cur, x_hbm,
                   wua_ref, wub_ref, wda_ref, wdb_ref, lng_ref, lnb_ref, o_hbm,
                   xbuf, obuf, gsem, ssem, *, alpha, pitch):
    i = pl.program_id(0)
    n = pl.num_programs(0)
    slot = i % 2
    ff = wda_ref.shape[1]
    d = wda_ref.shape[2]
    n_chunks = d // LANES
    x_rows = EXPERT_TILE * pitch
    o_rows = EXPERT_TILE * n_chunks

    n_used = nu_ref[0]

    @pl.when(i == 0)
    def _():
        _gather_tokens(src_cur, EXPERT_TILE, pitch, x_hbm, xbuf.at[0], gsem.at[0])

    @pl.when(i + 1 < n_used)
    def _():
        _gather_tokens(src_nxt, EXPERT_TILE, pitch, x_hbm, xbuf.at[1 - slot], gsem.at[1 - slot])

    @pl.when(i >= 2)
    def _():
        pltpu.make_async_copy(obuf.at[slot], o_hbm.at[pl.ds(0, o_rows)], ssem.at[slot]).wait()

    @pl.when(i < n_used)
    def _():
        pltpu.make_async_copy(x_hbm.at[pl.ds(0, x_rows)], xbuf.at[slot], gsem.at[slot]).wait()
        chunk = lambda c: xbuf[slot, pl.ds(c, EXPERT_TILE, stride=pitch), :]
        x = jnp.concatenate([chunk(c) for c in range(n_chunks)], axis=1)
        gates = chunk(n_chunks)
        xb = x.astype(bf16)

        ups = [jnp.dot(xb, wu_ref[0], preferred_element_type=f32) for wu_ref in (wua_ref, wub_ref)]
        acts = [(hh[:, :ff] * jax.nn.sigmoid(hh[:, :ff]) * hh[:, ff:]).astype(bf16) for hh in ups]
        y_a, y_b = [jnp.dot(act, wd_ref[0], preferred_element_type=f32)
                    for act, wd_ref in zip(acts, (wda_ref, wdb_ref))]
        y = gates[:, 0:1] * y_a + gates[:, 1:2] * y_b
        h2 = _layer_norm(alpha * x + y, lng_ref[...], lnb_ref[...])
        for c in range(n_chunks):
            obuf[slot, pl.ds(c, EXPERT_TILE, stride=n_chunks), :] = h2[:, c * LANES:(c + 1) * LANES]
        _scatter_tokens(dst_cur, EXPERT_TILE, n_chunks, obuf.at[slot], o_hbm, ssem.at[slot])

    @pl.when(i >= n_used)
    def _():
        pltpu.make_async_copy(obuf.at[slot], o_hbm.at[pl.ds(pl.multiple_of(i * o_rows, SUBLANES), o_rows)],
                              ssem.at[slot]).start()

    @pl.when(i == n - 1)
    def _():
        pltpu.make_async_copy(obuf.at[slot], o_hbm.at[pl.ds(0, o_rows)], ssem.at[slot]).wait()
        pltpu.make_async_copy(obuf.at[1 - slot], o_hbm.at[pl.ds(0, o_rows)], ssem.at[1 - slot]).wait()


def _experts(h1x, src, dst, tile_a, tile_b, n_used, wu, wd, layer, lng, lnb, alpha, n_tok_out):
    d = wd.shape[2]
    pitch = _chunk_pitch(d)
    n_chunks = d // LANES
    n_tiles = src.shape[0]
    ff2 = wu.shape[2]
    ff = wd.shape[1]
    e0 = layer * N_EXPERTS
    idx_spec = lambda f: pl.BlockSpec((1, 1, EXPERT_TILE), f, memory_space=pltpu.SMEM)
    const = lambda shape: pl.BlockSpec(shape, lambda i, ta, tb, nu: (0, 0))
    grid_spec = pltpu.PrefetchScalarGridSpec(
        num_scalar_prefetch=3,
        grid=(n_tiles,),
        in_specs=[
            idx_spec(lambda i, ta, tb, nu: (i, 0, 0)),
            idx_spec(lambda i, ta, tb, nu: (jnp.minimum(i + 1, n_tiles - 1), 0, 0)),
            idx_spec(lambda i, ta, tb, nu: (i, 0, 0)),
            pl.BlockSpec(memory_space=pl.ANY),
            pl.BlockSpec((1, d, ff2), lambda i, ta, tb, nu: (e0 + ta[i], 0, 0)),
            pl.BlockSpec((1, d, ff2), lambda i, ta, tb, nu: (e0 + tb[i], 0, 0)),
            pl.BlockSpec((1, ff, d), lambda i, ta, tb, nu: (e0 + ta[i], 0, 0)),
            pl.BlockSpec((1, ff, d), lambda i, ta, tb, nu: (e0 + tb[i], 0, 0)),
            const(lng.shape), const(lnb.shape),
        ],
        out_specs=pl.BlockSpec(memory_space=pl.ANY),
        scratch_shapes=[pltpu.VMEM((2, EXPERT_TILE * pitch, LANES), f32),
                        pltpu.VMEM((2, EXPERT_TILE * n_chunks, LANES), f32),
                        pltpu.SemaphoreType.DMA((2,)), pltpu.SemaphoreType.DMA((2,))],
    )
    return pl.pallas_call(
        functools.partial(_expert_kernel, alpha=alpha, pitch=pitch),
        grid_spec=grid_spec,
        out_shape=jax.ShapeDtypeStruct((n_tok_out * n_chunks, LANES), f32),
        compiler_params=pltpu.CompilerParams(
            dimension_semantics=("arbitrary",), vmem_limit_bytes=VMEM_LIMIT),
        name="experts_ln",
    )(tile_a, tile_b, n_used, src, src, dst, h1x, wu, wu, wd, wd, lng, lnb)


def _unchunk_kernel(h_ref, o_ref):
    tm, d = o_ref.shape
    n_chunks = d // LANES
    for c in range(n_chunks):
        o_ref[:, c * LANES:(c + 1) * LANES] = (
            h_ref[pl.ds(c, tm, stride=n_chunks), :].astype(o_ref.dtype))


def _unchunk(h, t, d, dtype, tm):
    n_chunks = d // LANES
    return pl.pallas_call(
        _unchunk_kernel,
        grid=(t // tm,),
        in_specs=[pl.BlockSpec((tm * n_chunks, LANES), lambda i: (i, 0))],
        out_specs=pl.BlockSpec((tm, d), lambda i: (i, 0)),
        out_shape=jax.ShapeDtypeStruct((t, d), dtype),
        compiler_params=pltpu.CompilerParams(
            dimension_semantics=("parallel",), vmem_limit_bytes=VMEM_LIMIT),
        name="unchunk",
    )(h)


def _rope_tables(positions):
    b, s = positions.shape
    half = ROPE_DIM // 2
    inv_freq = ROPE_THETA ** (-2.0 * jnp.arange(half, dtype=f32) / ROPE_DIM)
    ang = positions.astype(f32).reshape(b * s, 1) * inv_freq
    cos, sin = jnp.cos(ang), jnp.sin(ang)
    pad = LANES - ROPE_DIM
    cos_f = jnp.concatenate([cos, cos, jnp.ones((b * s, pad), f32)], axis=-1)
    sin_f = jnp.concatenate([-sin, sin, jnp.zeros((b * s, pad), f32)], axis=-1)
    return cos_f, sin_f


def _bucket_experts():
    a_ids, b_ids = [], []
    for g in range(N_EXPERT_GROUPS):
        for a in range(EXPERTS_PER_GROUP):
            for b in range(a + 1, EXPERTS_PER_GROUP):
                a_ids.append(g * EXPERTS_PER_GROUP + a)
                b_ids.append(g * EXPERTS_PER_GROUP + b)
    return jnp.array(a_ids, jnp.int32), jnp.array(b_ids, jnp.int32)


def _route_plan(route, cnt, t, n_tiles):
    bucket = route[:, 0]
    rank = route[:, 1]
    counts = cnt[0, :N_BUCKETS].astype(jnp.int32)
    padded = (counts + EXPERT_TILE - 1) // EXPERT_TILE * EXPERT_TILE
    ends = jnp.cumsum(padded)
    offs = ends - padded
    pos = offs[bucket] + rank
    n_used = (ends[-1] // EXPERT_TILE).astype(jnp.int32)
    tile_start = jnp.arange(n_tiles, dtype=jnp.int32) * EXPERT_TILE
    tile_bucket = jnp.sum((tile_start[:, None] >= ends[None, :]).astype(jnp.int32), axis=1)
    tile_bucket = jnp.minimum(tile_bucket, N_BUCKETS - 1)
    last_used = tile_bucket[jnp.maximum(n_used - 1, 0)]
    tile_bucket = jnp.where(jnp.arange(n_tiles) < n_used, tile_bucket, last_used)
    a_ids, b_ids = _bucket_experts()
    n_slots = n_tiles * EXPERT_TILE
    src = jnp.zeros((n_slots,), jnp.int32).at[pos].set(jnp.arange(t, dtype=jnp.int32))
    slot = jnp.arange(n_slots, dtype=jnp.int32)
    slot_bucket = jnp.repeat(tile_bucket, EXPERT_TILE)
    valid = (slot < ends[-1]) & (slot - offs[slot_bucket] < counts[slot_bucket])
    dummy = t + slot - jnp.cumsum(counts)[slot_bucket]
    dst = jnp.where(valid, src, dummy).astype(jnp.int32)
    shape = (n_tiles, 1, EXPERT_TILE)
    return (src.reshape(shape), dst.reshape(shape), a_ids[tile_bucket], b_ids[tile_bucket],
            n_used.reshape(1))


def kernel(x, positions, w_in, p_a, p_b, w_o, ln_g, ln_b, router_coarse, router_coarse_bias,
           router_fine, router_fine_bias, w_up, w_down):
    b, s, d = x.shape
    depth = w_in.shape[0]
    t = b * s
    alpha = float((2 * depth) ** 0.25)
    assert s % (DIL_CONFIGS[-1][1] * BLK) == 0 and s % SB_Q == 0 and d % GROUP_W == 0
    assert t % EXPERT_TILE == 0 and t // EXPERT_TILE >= 2

    sizes = (("ga", d), ("gb", d), ("qb", WIDTH_B), ("kb", WIDTH_B), ("vb", WIDTH_B))
    col, off = {}, 0
    for name, width in sizes:
        col[name] = off
        off += width
    a3, b3 = 3 * WIDTH_A, 3 * WIDTH_B

    rope = _rope_tables(positions)
    tm_proj = min(2048, s)
    tm_mix = min(256, t)
    n_tiles = t // EXPERT_TILE + N_BUCKETS
    n_chunks = d // LANES
    wu_all = w_up.astype(bf16).reshape((depth * N_EXPERTS,) + w_up.shape[2:])
    wd_all = w_down.astype(bf16).reshape((depth * N_EXPERTS,) + w_down.shape[2:])

    h = x.reshape(t, d)
    hb = h.astype(bf16)
    for l in range(depth):
        gate_blocks = 2 * d // GROUP_W
        main_block = lambda j: jnp.where(j < gate_blocks, (a3 + b3) // GROUP_W + j,
                                         a3 // GROUP_W + j - gate_blocks)
        main = _in_proj(hb, w_in, l, main_block, 2 * d + b3, b, s, 1, tm_proj, GROUP_W)
        main2 = main.reshape(t, main.shape[-1])
        outs, lses = [], []
        for g, (_, dilation) in enumerate(DIL_CONFIGS):
            group_block = lambda j, g=g: j * N_DIL_GROUPS + g
            qkv = _in_proj(hb, w_in, l, group_block, 3 * GROUP_W, b, s, dilation, tm_proj, GROUP_W,
                           rope=rope, n_rotary_tiles=2)
            o, lse = _dilated_attention(qkv, dilation)
            outs.append(o)
            lses.append(lse)
        o_b = _stick_breaking(main.reshape(b, s, main.shape[-1]), col)

        wr = jnp.zeros((d, LANES), f32)
        wr = wr.at[:, :N_EXPERT_GROUPS].set(router_coarse[l])
        wr = wr.at[:, ROUTER_LANE0:ROUTER_LANE0 + N_EXPERTS].set(router_fine[l]).astype(bf16)
        br = jnp.zeros((1, LANES), f32)
        br = br.at[0, :N_EXPERT_GROUPS].set(router_coarse_bias[l])
        br = br.at[0, ROUTER_LANE0:ROUTER_LANE0 + N_EXPERTS].set(router_fine_bias[l])
        h1x, route, cnt = _mix(
            outs, lses, o_b, main2, h, l > 0, p_a[l].astype(bf16), p_b[l].astype(bf16),
            w_o[l].astype(bf16), ln_g[l, 0].reshape(1, d), ln_b[l, 0].reshape(1, d), wr, br, col,
            alpha, tm_mix)

        src, dst, tile_a, tile_b, n_used = _route_plan(route, cnt, t, n_tiles)
        h = _experts(h1x, src * _chunk_pitch(d), dst * n_chunks, tile_a, tile_b, n_used, wu_all, wd_all,
                     l, ln_g[l, 1].reshape(1, d), ln_b[l, 1].reshape(1, d), alpha,
                     n_tiles * EXPERT_TILE)
        if l + 1 < depth:
            hb = _unchunk(h, t, d, bf16, min(512, t))
    return _unchunk(h, t, d, f32, min(512, t)).reshape(b, s, d)
```

```python
import functools

import jax
import jax.numpy as jnp
from jax import lax
from jax.experimental import pallas as pl
from jax.experimental.pallas import tpu as pltpu

HEAD_DIM = 128
DIL_CONFIGS = ((128, 1), (512, 4), (2048, 16))
N_DIL_GROUPS = 3
HEADS_PER_DIL_GROUP = 4
N_SB_HEADS = 8
ROPE_THETA = 500000.0
ROPE_DIM = HEAD_DIM // 4
N_EXPERT_GROUPS = 4
EXPERTS_PER_GROUP = 4
N_EXPERTS = N_EXPERT_GROUPS * EXPERTS_PER_GROUP
PAIRS_PER_GROUP = EXPERTS_PER_GROUP * (EXPERTS_PER_GROUP - 1) // 2
N_BUCKETS = N_EXPERT_GROUPS * PAIRS_PER_GROUP
LN_EPS = 1e-5

WIDTH_A = N_DIL_GROUPS * HEADS_PER_DIL_GROUP * HEAD_DIM
GROUP_W = HEADS_PER_DIL_GROUP * HEAD_DIM
WIDTH_B = N_SB_HEADS * HEAD_DIM

BLK = 128
LANES = 128
SUBLANES = 8
DIL_RES_PER_STEP = 4
SB_Q = 256
SB_K = 256
SB_HEADS_PER_STEP = 4
LOG2_E = 1.4426950408889634
SB_DEAD_LOG2 = -151.0
ROUTER_LANE0 = N_EXPERT_GROUPS
EXPERT_TILE = 256
IN_PROJ_ROW_CHUNKS = 8
VMEM_LIMIT = 56 * 1024 * 1024
NEG_BIG = -1e30

f32 = jnp.float32
bf16 = jnp.bfloat16


def _rotate(t, cos_f, sin_f, lane):
    swapped = jnp.where(lane < ROPE_DIM // 2,
                        pltpu.roll(t, LANES - ROPE_DIM // 2, 1),
                        pltpu.roll(t, ROPE_DIM // 2, 1))
    return t * cos_f + swapped * sin_f


def _in_proj_kernel(x_ref, w_ref, *rest, dilation, n_rotary_tiles):
    if n_rotary_tiles:
        cos_ref, sin_ref, o_ref, *scratch = rest
    else:
        o_ref, *scratch = rest
    tm = x_ref.shape[0]
    w = w_ref[0].astype(bf16)
    if dilation == 1 and not n_rotary_tiles:
        o_ref[0, 0] = jnp.dot(x_ref[...], w, preferred_element_type=f32).astype(o_ref.dtype)
        return

    n_chunks = IN_PROJ_ROW_CHUNKS if tm % (IN_PROJ_ROW_CHUNKS * dilation * SUBLANES) == 0 else 1
    rows = tm // n_chunks
    n_slabs = w.shape[1] // LANES

    def matmul(c):
        return jnp.dot(x_ref[c * rows:(c + 1) * rows, :], w, preferred_element_type=f32)

    def epilogue(c, acc):
        slabs = [acc[:, s * LANES:(s + 1) * LANES] for s in range(n_slabs)]
        if n_rotary_tiles:
            cos_f = cos_ref[c * rows:(c + 1) * rows, :]
            sin_f = sin_ref[c * rows:(c + 1) * rows, :]
            lane = lax.broadcasted_iota(jnp.int32, cos_f.shape, 1)
            is_qk = pl.program_id(1) < n_rotary_tiles
            slabs = [jnp.where(is_qk, _rotate(slab, cos_f, sin_f, lane), slab) for slab in slabs]
        if dilation == 1:
            o_ref[0, 0, c * rows:(c + 1) * rows, :] = jnp.concatenate(slabs, axis=1).astype(o_ref.dtype)
            return
        (acc_ref,) = scratch
        n = rows // dilation
        for s, slab in enumerate(slabs):
            acc_ref[s, c * rows:(c + 1) * rows, :] = slab
        for r in range(dilation):
            for s in range(n_slabs):
                o_ref[0, r, c * n:(c + 1) * n, s * LANES:(s + 1) * LANES] = (
                    acc_ref[s, pl.ds(c * rows + r, n, stride=dilation), :].astype(o_ref.dtype))

    acc = matmul(0)
    for c in range(n_chunks):
        nxt = matmul(c + 1) if c + 1 < n_chunks else None
        epilogue(c, acc)
        acc = nxt


def _in_proj(x, w_in, layer, col_block, n_out, b, s, dilation, tm, tn, rope=None, n_rotary_tiles=0):
    k = x.shape[1]
    per_b = s // tm
    scratch = [] if dilation == 1 else [pltpu.VMEM((tn // LANES, tm, LANES), f32)]
    tables = list(rope) if n_rotary_tiles else []
    return pl.pallas_call(
        functools.partial(_in_proj_kernel, dilation=dilation, n_rotary_tiles=n_rotary_tiles),
        grid=(b * per_b, n_out // tn),
        in_specs=[pl.BlockSpec((tm, k), lambda i, j: (i, 0)),
                  pl.BlockSpec((1, k, tn), lambda i, j: (layer, 0, col_block(j)))]
                 + [pl.BlockSpec((tm, LANES), lambda i, j: (i, 0))] * len(tables),
        out_specs=pl.BlockSpec((1, dilation, tm // dilation, tn),
                               lambda i, j: (i // per_b, 0, i % per_b, j)),
        out_shape=jax.ShapeDtypeStruct((b, dilation, s // dilation, n_out), bf16),
        scratch_shapes=scratch,
        compiler_params=pltpu.CompilerParams(
            dimension_semantics=("parallel", "parallel"), vmem_limit_bytes=VMEM_LIMIT),
        name=f"in_proj_d{dilation}",
    )(x, w_in, *tables)


def _dil_kernel(q_ref, kc_ref, kp_ref, vc_ref, vp_ref, o_ref, lse_ref, o_scr, lse_scr,
                *, dilation, res_per_step):
    n = pl.program_id(1)
    rr = pl.program_id(2)
    qi = lax.broadcasted_iota(jnp.int32, (BLK, 2 * BLK), 0)
    kj = lax.broadcasted_iota(jnp.int32, (BLK, 2 * BLK), 1)
    first_key = jnp.where(n > 0, 0, BLK)
    valid = (kj >= jnp.maximum(qi, first_key)) & (kj <= qi + BLK)
    scale = HEAD_DIM ** -0.5

    def residue(ri, _):
        rows = pl.ds(rr * res_per_step + ri, BLK, stride=dilation) if dilation > 1 else slice(None)
        heads = range(HEADS_PER_DIL_GROUP)
        sls = [slice(h * HEAD_DIM, (h + 1) * HEAD_DIM) for h in heads]
        ss = [jnp.where(valid,
                        lax.dot_general(q_ref[0, ri, :, sl],
                                        jnp.concatenate([kp_ref[0, ri, :, sl], kc_ref[0, ri, :, sl]], axis=0),
                                        (((1,), (1,)), ((), ())), preferred_element_type=f32) * scale,
                        NEG_BIG)
              for sl in sls]
        ms = [jnp.max(s, axis=-1, keepdims=True) for s in ss]
        ps = [jnp.exp(s - m) for s, m in zip(ss, ms)]
        dens = [jnp.sum(p, axis=-1, keepdims=True) for p in ps]
        os_ = [jnp.dot(p.astype(bf16),
                       jnp.concatenate([vp_ref[0, ri, :, sl], vc_ref[0, ri, :, sl]], axis=0),
                       preferred_element_type=f32) for p, sl in zip(ps, sls)]
        for h in heads:
            o_scr[h, rows, :] = os_[h] / dens[h]
            lse_scr[h, rows, :] = jnp.broadcast_to(ms[h] + jnp.log(dens[h]), (BLK, HEAD_DIM))
        return 0

    lax.fori_loop(0, res_per_step, residue, 0)

    @pl.when(rr == pl.num_programs(2) - 1)
    def _():
        for h in range(HEADS_PER_DIL_GROUP):
            sl = slice(h * HEAD_DIM, (h + 1) * HEAD_DIM)
            o_ref[0, :, sl] = o_scr[h].astype(o_ref.dtype)
            lse_ref[0, :, sl] = lse_scr[h]


def _dilated_attention(qkv, dilation):
    b, _, sub, _ = qkv.shape
    s = sub * dilation
    nb = sub // BLK
    rs = min(dilation, DIL_RES_PER_STEP)

    def cur(c):
        return pl.BlockSpec((1, rs, BLK, GROUP_W), lambda bi, n, rr: (bi, rr, n, c))

    def prev(c):
        return pl.BlockSpec((1, rs, BLK, GROUP_W), lambda bi, n, rr: (bi, rr, jnp.maximum(n - 1, 0), c))

    out_spec = pl.BlockSpec((1, dilation * BLK, GROUP_W), lambda bi, n, rr: (bi, n, 0))
    o, lse = pl.pallas_call(
        functools.partial(_dil_kernel, dilation=dilation, res_per_step=rs),
        grid=(b, nb, dilation // rs),
        in_specs=[cur(0), cur(1), prev(1), cur(2), prev(2)],
        out_specs=[out_spec, out_spec],
        out_shape=[jax.ShapeDtypeStruct((b, s, GROUP_W), bf16),
                   jax.ShapeDtypeStruct((b, s, GROUP_W), f32)],
        scratch_shapes=[pltpu.VMEM((HEADS_PER_DIL_GROUP, dilation * BLK, HEAD_DIM), f32),
                        pltpu.VMEM((HEADS_PER_DIL_GROUP, dilation * BLK, HEAD_DIM), f32)],
        compiler_params=pltpu.CompilerParams(
            dimension_semantics=("parallel", "arbitrary", "arbitrary"), vmem_limit_bytes=VMEM_LIMIT),
        name=f"dilated_attn_d{dilation}",
    )(qkv, qkv, qkv, qkv, qkv)
    return o.reshape(b * s, GROUP_W), lse.reshape(b * s, GROUP_W)


def _sb_kernel(q_ref, k_ref, v_ref, up_ref, o_ref, acc_ref, c_ref):
    i = pl.program_id(2)
    scale2 = HEAD_DIM ** -0.5 * LOG2_E
    row = lax.broadcasted_iota(jnp.int32, (SB_Q, SB_K), 0)
    coli = lax.broadcasted_iota(jnp.int32, (SB_Q, SB_K), 1)
    causal = coli < row
    acc_ref[...] = jnp.zeros_like(acc_ref)
    c_ref[...] = jnp.zeros_like(c_ref)

    def chunk(j, diag):
        start = pl.multiple_of(j * SB_K, SB_K)
        heads = range(SB_HEADS_PER_STEP)
        sls = [slice(hd * HEAD_DIM, (hd + 1) * HEAD_DIM) for hd in heads]
        ts = [lax.dot_general(q_ref[0, :, sl], k_ref[0, pl.ds(start, SB_K), sl],
                              (((1,), (1,)), ((), ())), preferred_element_type=f32) * scale2
              for sl in sls]
        lss, lks, splits = [], [], []
        for t in ts:
            ls = jnp.minimum(t, 0.0) - jnp.log2(1.0 + jnp.exp2(-jnp.abs(t)))
            lk = ls - t
            if diag:
                lk = jnp.where(causal, lk, 0.0)
            hi = lk.astype(bf16)
            lo = (lk - hi.astype(f32)).astype(bf16)
            lss.append(ls)
            lks.append(lk)
            splits.append(jnp.concatenate([hi, lo], axis=1))
        laters = [jnp.dot(x, up_ref[...], preferred_element_type=f32) for x in splits]
        probs = []
        for hd in heads:
            c = c_ref[hd]
            a = jnp.exp2(lss[hd] + laters[hd] + jnp.concatenate([c, c], axis=1))
            if diag:
                a = jnp.where(causal, a, 0.0)
            probs.append(a.astype(bf16))
            c_ref[hd] = c + jnp.sum(lks[hd], axis=1, keepdims=True)
        for hd in heads:
            acc_ref[hd] += jnp.dot(probs[hd], v_ref[0, pl.ds(start, SB_K), sls[hd]],
                                   preferred_element_type=f32)

    chunk(i, True)

    def live():
        return jnp.max(c_ref[...]) > SB_DEAD_LOG2

    def cond(state):
        t, go = state
        return jnp.logical_and(t < i, go)

    def body(state):
        t, _ = state
        chunk(i - 1 - t, False)
        return t + 1, live()

    lax.while_loop(cond, body, (jnp.int32(0), live()))
    for hd in range(SB_HEADS_PER_STEP):
        o_ref[0, :, hd * HEAD_DIM:(hd + 1) * HEAD_DIM] = acc_ref[hd].astype(o_ref.dtype)


def _stick_breaking(main3, col):
    b, s, _ = main3.shape
    w = SB_HEADS_PER_STEP * HEAD_DIM
    qb, kb, vb = col["qb"] // w, col["kb"] // w, col["vb"] // w
    jr = lax.broadcasted_iota(jnp.int32, (2 * SB_K, SB_K), 0)
    sc = lax.broadcasted_iota(jnp.int32, (2 * SB_K, SB_K), 1)
    upper2 = jnp.where((jr % SB_K) > sc, 1.0, 0.0).astype(bf16)
    o = pl.pallas_call(
        _sb_kernel,
        grid=(b, N_SB_HEADS // SB_HEADS_PER_STEP, s // SB_Q),
        in_specs=[pl.BlockSpec((1, SB_Q, w), lambda bi, h, i: (bi, i, qb + h)),
                  pl.BlockSpec((1, s, w), lambda bi, h, i: (bi, 0, kb + h)),
                  pl.BlockSpec((1, s, w), lambda bi, h, i: (bi, 0, vb + h)),
                  pl.BlockSpec((2 * SB_K, SB_K), lambda bi, h, i: (0, 0))],
        out_specs=pl.BlockSpec((1, SB_Q, w), lambda bi, h, i: (bi, i, h)),
        out_shape=jax.ShapeDtypeStruct((b, s, WIDTH_B), bf16),
        scratch_shapes=[pltpu.VMEM((SB_HEADS_PER_STEP, SB_Q, HEAD_DIM), f32),
                        pltpu.VMEM((SB_HEADS_PER_STEP, SB_Q, HEAD_DIM), f32)],
        compiler_params=pltpu.CompilerParams(
            dimension_semantics=("parallel", "parallel", "arbitrary"), vmem_limit_bytes=VMEM_LIMIT),
        name="stick_breaking_attn",
    )(main3, main3, main3, upper2)
    return o.reshape(b * s, WIDTH_B)


def _layer_norm(u, g, b):
    mu = jnp.mean(u, axis=-1, keepdims=True)
    d = u - mu
    var = jnp.mean(d * d, axis=-1, keepdims=True)
    return d * lax.rsqrt(var + LN_EPS) * g + b


def _mix_kernel(o0_ref, o1_ref, o2_ref, l0_ref, l1_ref, l2_ref, ob_ref, ga_ref, gb_ref, h_ref,
                pa_ref, pb_ref, wo_ref, lng_ref, lnb_ref, wr_ref, br_ref,
                h1_ref, route_ref, cnt_ref, carry_ref, *, alpha, h_chunked):
    step = pl.program_id(0)

    @pl.when(step == 0)
    def _():
        carry_ref[...] = jnp.zeros_like(carry_ref)

    l0, l1, l2 = l0_ref[...], l1_ref[...], l2_ref[...]
    m = jnp.maximum(jnp.maximum(l0, l1), l2)
    e0, e1, e2 = jnp.exp(l0 - m), jnp.exp(l1 - m), jnp.exp(l2 - m)
    o_a = (e0 * o0_ref[...].astype(f32) + e1 * o1_ref[...].astype(f32)
           + e2 * o2_ref[...].astype(f32)) / (e0 + e1 + e2)
    ya = jnp.dot(o_a.astype(bf16), pa_ref[...], preferred_element_type=f32)
    yb = jnp.dot(ob_ref[...], pb_ref[...], preferred_element_type=f32)
    merged = (jax.nn.sigmoid(ga_ref[...].astype(f32)) * ya
              + jax.nn.sigmoid(gb_ref[...].astype(f32)) * yb)
    y = jnp.dot(merged.astype(bf16), wo_ref[...], preferred_element_type=f32)
    tm, d_model = y.shape
    n_chunks = d_model // LANES
    if h_chunked:
        h_res = jnp.concatenate(
            [h_ref[pl.ds(c, tm, stride=n_chunks), :] for c in range(n_chunks)], axis=1)
    else:
        h_res = h_ref[...]
    h1 = _layer_norm(alpha * h_res + y, lng_ref[...], lnb_ref[...])
    pitch = h1_ref.shape[0] // tm
    for c in range(n_chunks):
        h1_ref[pl.ds(c, tm, stride=pitch), :] = h1[:, c * LANES:(c + 1) * LANES]
    for c in range(n_chunks + 1, pitch):
        h1_ref[pl.ds(c, tm, stride=pitch), :] = jnp.zeros((tm, LANES), f32)
    h1b = h1.astype(bf16)

    logits = jnp.dot(h1b, wr_ref[...], preferred_element_type=f32) + br_ref[...]
    lane = lax.broadcasted_iota(jnp.int32, (tm, LANES), 1).astype(f32)
    no_lane = float(LANES)
    lc = jnp.where(lane < N_EXPERT_GROUPS, logits, NEG_BIG)
    mc = jnp.max(lc, axis=-1, keepdims=True)
    p_grp = 1.0 / jnp.sum(jnp.exp(lc - mc), axis=-1, keepdims=True)
    g_idx = jnp.min(jnp.where(lc == mc, lane, no_lane), axis=-1, keepdims=True)
    lo_lane = ROUTER_LANE0 + EXPERTS_PER_GROUP * g_idx
    sel = (lane >= lo_lane) & (lane < lo_lane + EXPERTS_PER_GROUP)
    lf = jnp.where(sel, logits, NEG_BIG)
    v1 = jnp.max(lf, axis=-1, keepdims=True)
    i1 = jnp.min(jnp.where(lf == v1, lane, no_lane), axis=-1, keepdims=True)
    lf2 = jnp.where(lane == i1, NEG_BIG, lf)
    v2 = jnp.max(lf2, axis=-1, keepdims=True)
    i2 = jnp.min(jnp.where((lf2 == v2) & sel & (lane != i1), lane, no_lane), axis=-1, keepdims=True)
    t = jnp.exp(v2 - v1)
    w1 = 1.0 / (1.0 + t)
    w2 = t / (1.0 + t)
    e1 = i1 - lo_lane
    e2 = i2 - lo_lane
    a = jnp.minimum(e1, e2)
    b = jnp.maximum(e1, e2)
    pair = a * (2.0 * EXPERTS_PER_GROUP - 1.0 - a) * 0.5 + (b - a - 1.0)
    bucket = PAIRS_PER_GROUP * g_idx + pair
    first_is_a = e1 < e2
    g1 = p_grp * w1
    g2 = p_grp * w2
    w_a = jnp.where(first_is_a, g1, g2)
    w_b = jnp.where(first_is_a, g2, g1)
    h1_ref[pl.ds(n_chunks, tm, stride=pitch), :] = jnp.where(
        lane == 0, w_a, jnp.where(lane == 1, w_b, 0.0))

    hit = lane == bucket
    onehot = jnp.where(hit, 1.0, 0.0)
    r = lax.broadcasted_iota(jnp.int32, (tm, tm), 0)
    c = lax.broadcasted_iota(jnp.int32, (tm, tm), 1)
    lower = jnp.where(c < r, 1.0, 0.0).astype(bf16)
    before = jnp.dot(lower, onehot.astype(bf16), preferred_element_type=f32) + carry_ref[0:1, :]
    rank = jnp.sum(jnp.where(hit, before, 0.0), axis=-1, keepdims=True)
    route = jnp.where(lane == 0, bucket, jnp.where(lane == 1, rank, 0.0))
    route_ref[...] = route.astype(jnp.int32)
    new_carry = carry_ref[0:1, :] + jnp.sum(onehot, axis=0, keepdims=True)
    carry_ref[...] = jnp.broadcast_to(new_carry, carry_ref.shape)
    cnt_ref[...] = jnp.broadcast_to(new_carry, cnt_ref.shape)


def _chunk_pitch(d):
    return -(-(d // LANES + 1) // SUBLANES) * SUBLANES


def _mix(outs, lses, o_b, main2, h, h_chunked, pa, pb, wo, lng, lnb, wr, br, col, alpha, tm):
    t = main2.shape[0]
    d = pa.shape[1]
    pitch = _chunk_pitch(d)
    row = lambda width, c: pl.BlockSpec((tm, width), lambda i: (i, c))
    const = lambda shape: pl.BlockSpec(shape, lambda i: (0, 0), pipeline_mode=pl.Buffered(1))
    h_spec = pl.BlockSpec((tm * (d // LANES), LANES), lambda i: (i, 0)) if h_chunked else row(d, 0)
    in_specs = ([row(GROUP_W, 0)] * 6
                + [row(WIDTH_B, 0), row(d, col["ga"] // d), row(d, col["gb"] // d), h_spec,
                   const(pa.shape), const(pb.shape), const(wo.shape),
                   const(lng.shape), const(lnb.shape), const(wr.shape), const(br.shape)])
    return pl.pallas_call(
        functools.partial(_mix_kernel, alpha=alpha, h_chunked=h_chunked),
        grid=(t // tm,),
        in_specs=in_specs,
        out_specs=[pl.BlockSpec((tm * pitch, LANES), lambda i: (i, 0)), row(LANES, 0),
                   pl.BlockSpec((8, LANES), lambda i: (0, 0))],
        out_shape=[jax.ShapeDtypeStruct((t * pitch, LANES), f32),
                   jax.ShapeDtypeStruct((t, LANES), jnp.int32),
                   jax.ShapeDtypeStruct((8, LANES), f32)],
        scratch_shapes=[pltpu.VMEM((8, LANES), f32)],
        compiler_params=pltpu.CompilerParams(
            dimension_semantics=("arbitrary",), vmem_limit_bytes=VMEM_LIMIT),
        name="mix_ln_router",
    )(*outs, *lses, o_b, main2, main2, h, pa, pb, wo, lng, lnb, wr, br)


def _gather_tokens(idx_ref, n_tok, pitch, src_hbm, dst, sem):
    def body(p, _):
        for k in range(2):
            tok = 2 * p + k
            start = pl.multiple_of(idx_ref[0, 0, tok], SUBLANES)
            pltpu.make_async_copy(src_hbm.at[pl.ds(start, pitch)],
                                  dst.at[pl.ds(pl.multiple_of(tok * pitch, SUBLANES), pitch)],
                                  sem).start(priority=k)
        return 0
    lax.fori_loop(0, n_tok // 2, body, 0, unroll=4)


def _scatter_tokens(idx_ref, n_tok, pitch, src, dst_hbm, sem):
    def body(p, _):
        for k in range(2):
            tok = 2 * p + k
            start = pl.multiple_of(idx_ref[0, 0, tok], SUBLANES)
            pltpu.make_async_copy(src.at[pl.ds(pl.multiple_of(tok * pitch, SUBLANES), pitch)],
                                  dst_hbm.at[pl.ds(start, pitch)], sem).start(priority=k)
        return 0
    lax.fori_loop(0, n_tok // 2, body, 0, unroll=4)


def _expert_kernel(ta_ref, tb_ref, nu_ref, src_cur, src_nxt, dst_cur, x_hbm,
                   wua_ref, wub_ref, wda_ref, wdb_ref, lng_ref, lnb_ref, o_hbm,
                   xbuf, obuf, gsem, ssem, *, alpha, pitch):
    i = pl.program_id(0)
    n = pl.num_programs(0)
    slot = i % 2
    ff = wda_ref.shape[1]
    d = wda_ref.shape[2]
    n_chunks = d // LANES
    x_rows = EXPERT_TILE * pitch
    o_rows = EXPERT_TILE * n_chunks

    n_used = nu_ref[0]

    @pl.when(i == 0)
    def _():
        _gather_tokens(src_cur, EXPERT_TILE, pitch, x_hbm, xbuf.at[0], gsem.at[0])

    @pl.when(i + 1 < n_used)
    def _():
        _gather_tokens(src_nxt, EXPERT_TILE, pitch, x_hbm, xbuf.at[1 - slot], gsem.at[1 - slot])

    @pl.when(i >= 2)
    def _():
        pltpu.make_async_copy(obuf.at[slot], o_hbm.at[pl.ds(0, o_rows)], ssem.at[slot]).wait()

    @pl.when(i < n_used)
    def _():
        pltpu.make_async_copy(x_hbm.at[pl.ds(0, x_rows)], xbuf.at[slot], gsem.at[slot]).wait()
        chunk = lambda c: xbuf[slot, pl.ds(c, EXPERT_TILE, stride=pitch), :]
        x = jnp.concatenate([chunk(c) for c in range(n_chunks)], axis=1)
        gates = chunk(n_chunks)
        xb = x.astype(bf16)

        ups = [jnp.dot(xb, wu_ref[0], preferred_element_type=f32) for wu_ref in (wua_ref, wub_ref)]
        acts = [(hh[:, :ff] * jax.nn.sigmoid(hh[:, :ff]) * hh[:, ff:]).astype(bf16) for hh in ups]
        y_a, y_b = [jnp.dot(act, wd_ref[0], preferred_element_type=f32)
                    for act, wd_ref in zip(acts, (wda_ref, wdb_ref))]
        y = gates[:, 0:1] * y_a + gates[:, 1:2] * y_b
        h2 = _layer_norm(alpha * x + y, lng_ref[...], lnb_ref[...])
        for c in range(n_chunks):
            obuf[slot, pl.ds(c, EXPERT_TILE, stride=n_chunks), :] = h2[:, c * LANES:(c + 1) * LANES]
        _scatter_tokens(dst_cur, EXPERT_TILE, n_chunks, obuf.at[slot], o_hbm, ssem.at[slot])

    @pl.when(i >= n_used)
    def _():
        pltpu.make_async_copy(obuf.at[slot], o_hbm.at[pl.ds(pl.multiple_of(i * o_rows, SUBLANES), o_rows)],
                              ssem.at[slot]).start()

    @pl.when(i == n - 1)
    def _():
        pltpu.make_async_copy(obuf.at[slot], o_hbm.at[pl.ds(0, o_rows)], ssem.at[slot]).wait()
        pltpu.make_async_copy(obuf.at[1 - slot], o_hbm.at[pl.ds(0, o_rows)], ssem.at[1 - slot]).wait()


def _experts(h1x, src, dst, tile_a, tile_b, n_used, wu, wd, layer, lng, lnb, alpha, n_tok_out):
    d = wd.shape[2]
    pitch = _chunk_pitch(d)
    n_chunks = d // LANES
    n_tiles = src.shape[0]
    ff2 = wu.shape[2]
    ff = wd.shape[1]
    e0 = layer * N_EXPERTS
    idx_spec = lambda f: pl.BlockSpec((1, 1, EXPERT_TILE), f, memory_space=pltpu.SMEM)
    const = lambda shape: pl.BlockSpec(shape, lambda i, ta, tb, nu: (0, 0))
    grid_spec = pltpu.PrefetchScalarGridSpec(
        num_scalar_prefetch=3,
        grid=(n_tiles,),
        in_specs=[
            idx_spec(lambda i, ta, tb, nu: (i, 0, 0)),
            idx_spec(lambda i, ta, tb, nu: (jnp.minimum(i + 1, n_tiles - 1), 0, 0)),
            idx_spec(lambda i, ta, tb, nu: (i, 0, 0)),
            pl.BlockSpec(memory_space=pl.ANY),
            pl.BlockSpec((1, d, ff2), lambda i, ta, tb, nu: (e0 + ta[i], 0, 0)),
            pl.BlockSpec((1, d, ff2), lambda i, ta, tb, nu: (e0 + tb[i], 0, 0)),
            pl.BlockSpec((1, ff, d), lambda i, ta, tb, nu: (e0 + ta[i], 0, 0)),
            pl.BlockSpec((1, ff, d), lambda i, ta, tb, nu: (e0 + tb[i], 0, 0)),
            const(lng.shape), const(lnb.shape),
        ],
        out_specs=pl.BlockSpec(memory_space=pl.ANY),
        scratch_shapes=[pltpu.VMEM((2, EXPERT_TILE * pitch, LANES), f32),
                        pltpu.VMEM((2, EXPERT_TILE * n_chunks, LANES), f32),
                        pltpu.SemaphoreType.DMA((2,)), pltpu.SemaphoreType.DMA((2,))],
    )
    return pl.pallas_call(
        functools.partial(_expert_kernel, alpha=alpha, pitch=pitch),
        grid_spec=grid_spec,
        out_shape=jax.ShapeDtypeStruct((n_tok_out * n_chunks, LANES), f32),
        compiler_params=pltpu.CompilerParams(
            dimension_semantics=("arbitrary",), vmem_limit_bytes=VMEM_LIMIT),
        name="experts_ln",
    )(tile_a, tile_b, n_used, src, src, dst, h1x, wu, wu, wd, wd, lng, lnb)


def _unchunk_kernel(h_ref, o_ref):
    tm, d = o_ref.shape
    n_chunks = d // LANES
    for c in range(n_chunks):
        o_ref[:, c * LANES:(c + 1) * LANES] = (
            h_ref[pl.ds(c, tm, stride=n_chunks), :].astype(o_ref.dtype))


def _unchunk(h, t, d, dtype, tm):
    n_chunks = d // LANES
    return pl.pallas_call(
        _unchunk_kernel,
        grid=(t // tm,),
        in_specs=[pl.BlockSpec((tm * n_chunks, LANES), lambda i: (i, 0))],
        out_specs=pl.BlockSpec((tm, d), lambda i: (i, 0)),
        out_shape=jax.ShapeDtypeStruct((t, d), dtype),
        compiler_params=pltpu.CompilerParams(
            dimension_semantics=("parallel",), vmem_limit_bytes=VMEM_LIMIT),
        name="unchunk",
    )(h)


def _rope_tables(positions):
    b, s = positions.shape
    half = ROPE_DIM // 2
    inv_freq = ROPE_THETA ** (-2.0 * jnp.arange(half, dtype=f32) / ROPE_DIM)
    ang = positions.astype(f32).reshape(b * s, 1) * inv_freq
    cos, sin = jnp.cos(ang), jnp.sin(ang)
    pad = LANES - ROPE_DIM
    cos_f = jnp.concatenate([cos, cos, jnp.ones((b * s, pad), f32)], axis=-1)
    sin_f = jnp.concatenate([-sin, sin, jnp.zeros((b * s, pad), f32)], axis=-1)
    return cos_f, sin_f


def _bucket_experts():
    a_ids, b_ids = [], []
    for g in range(N_EXPERT_GROUPS):
        for a in range(EXPERTS_PER_GROUP):
            for b in range(a + 1, EXPERTS_PER_GROUP):
                a_ids.append(g * EXPERTS_PER_GROUP + a)
                b_ids.append(g * EXPERTS_PER_GROUP + b)
    return jnp.array(a_ids, jnp.int32), jnp.array(b_ids, jnp.int32)


def _route_plan(route, cnt, t, n_tiles):
    bucket = route[:, 0]
    rank = route[:, 1]
    counts = cnt[0, :N_BUCKETS].astype(jnp.int32)
    padded = (counts + EXPERT_TILE - 1) // EXPERT_TILE * EXPERT_TILE
    ends = jnp.cumsum(padded)
    offs = ends - padded
    pos = offs[bucket] + rank
    n_used = (ends[-1] // EXPERT_TILE).astype(jnp.int32)
    tile_start = jnp.arange(n_tiles, dtype=jnp.int32) * EXPERT_TILE
    tile_bucket = jnp.sum((tile_start[:, None] >= ends[None, :]).astype(jnp.int32), axis=1)
    tile_bucket = jnp.minimum(tile_bucket, N_BUCKETS - 1)
    last_used = tile_bucket[jnp.maximum(n_used - 1, 0)]
    tile_bucket = jnp.where(jnp.arange(n_tiles) < n_used, tile_bucket, last_used)
    a_ids, b_ids = _bucket_experts()
    n_slots = n_tiles * EXPERT_TILE
    slot = jnp.arange(n_slots, dtype=jnp.int32)
    src = (slot % t).at[pos].set(jnp.arange(t, dtype=jnp.int32))
    slot_bucket = jnp.repeat(tile_bucket, EXPERT_TILE)
    valid = (slot < ends[-1]) & (slot - offs[slot_bucket] < counts[slot_bucket])
    dummy = t + slot - jnp.cumsum(counts)[slot_bucket]
    dst = jnp.where(valid, src, dummy).astype(jnp.int32)
    shape = (n_tiles, 1, EXPERT_TILE)
    return (src.reshape(shape), dst.reshape(shape), a_ids[tile_bucket], b_ids[tile_bucket],
            n_used.reshape(1))


def kernel(x, positions, w_in, p_a, p_b, w_o, ln_g, ln_b, router_coarse, router_coarse_bias,
           router_fine, router_fine_bias, w_up, w_down):
    b, s, d = x.shape
    depth = w_in.shape[0]
    t = b * s
    alpha = float((2 * depth) ** 0.25)
    assert s % (DIL_CONFIGS[-1][1] * BLK) == 0 and s % SB_Q == 0 and d % GROUP_W == 0
    assert t % EXPERT_TILE == 0 and t // EXPERT_TILE >= 2

    sizes = (("ga", d), ("gb", d), ("qb", WIDTH_B), ("kb", WIDTH_B), ("vb", WIDTH_B))
    col, off = {}, 0
    for name, width in sizes:
        col[name] = off
        off += width
    a3, b3 = 3 * WIDTH_A, 3 * WIDTH_B

    rope = _rope_tables(positions)
    tm_proj = min(2048, s)
    tm_mix = min(256, t)
    n_tiles = t // EXPERT_TILE + N_BUCKETS
    n_chunks = d // LANES
    wu_all = w_up.astype(bf16).reshape((depth * N_EXPERTS,) + w_up.shape[2:])
    wd_all = w_down.astype(bf16).reshape((depth * N_EXPERTS,) + w_down.shape[2:])

    h = x.reshape(t, d)
    hb = h.astype(bf16)
    for l in range(depth):
        gate_blocks = 2 * d // GROUP_W
        main_block = lambda j: jnp.where(j < gate_blocks, (a3 + b3) // GROUP_W + j,
                                         a3 // GROUP_W + j - gate_blocks)
        main = _in_proj(hb, w_in, l, main_block, 2 * d + b3, b, s, 1, tm_proj, GROUP_W)
        main2 = main.reshape(t, main.shape[-1])
        outs, lses = [], []
        for g, (_, dilation) in enumerate(DIL_CONFIGS):
            group_block = lambda j, g=g: j * N_DIL_GROUPS + g
            qkv = _in_proj(hb, w_in, l, group_block, 3 * GROUP_W, b, s, dilation, tm_proj, GROUP_W,
                           rope=rope, n_rotary_tiles=2)
            o, lse = _dilated_attention(qkv, dilation)
            outs.append(o)
            lses.append(lse)
        o_b = _stick_breaking(main.reshape(b, s, main.shape[-1]), col)

        wr = jnp.zeros((d, LANES), f32)
        wr = wr.at[:, :N_EXPERT_GROUPS].set(router_coarse[l])
        wr = wr.at[:, ROUTER_LANE0:ROUTER_LANE0 + N_EXPERTS].set(router_fine[l]).astype(bf16)
        br = jnp.zeros((1, LANES), f32)
        br = br.at[0, :N_EXPERT_GROUPS].set(router_coarse_bias[l])
        br = br.at[0, ROUTER_LANE0:ROUTER_LANE0 + N_EXPERTS].set(router_fine_bias[l])
        h1x, route, cnt = _mix(
            outs, lses, o_b, main2, h, l > 0, p_a[l].astype(bf16), p_b[l].astype(bf16),
            w_o[l].astype(bf16), ln_g[l, 0].reshape(1, d), ln_b[l, 0].reshape(1, d), wr, br, col,
            alpha, tm_mix)

        src, dst, tile_a, tile_b, n_used = _route_plan(route, cnt, t, n_tiles)
        h = _experts(h1x, src * _chunk_pitch(d), dst * n_chunks, tile_a, tile_b, n_used, wu_all, wd_all,
                     l, ln_g[l, 1].reshape(1, d), ln_b[l, 1].reshape(1, d), alpha,
                     n_tiles * EXPERT_TILE)
        if l + 1 < depth:
            hb = _unchunk(h, t, d, bf16, min(512, t))
    return _unchunk(h, t, d, f32, min(512, t)).reshape(b, s, d)
```

```python
import functools

import jax
import jax.numpy as jnp
from jax import lax
from jax.experimental import pallas as pl
from jax.experimental.pallas import tpu as pltpu

HEAD_DIM = 128
DIL_CONFIGS = ((128, 1), (512, 4), (2048, 16))
N_DIL_GROUPS = 3
HEADS_PER_DIL_GROUP = 4
N_SB_HEADS = 8
ROPE_THETA = 500000.0
ROPE_DIM = HEAD_DIM // 4
N_EXPERT_GROUPS = 4
EXPERTS_PER_GROUP = 4
N_EXPERTS = N_EXPERT_GROUPS * EXPERTS_PER_GROUP
PAIRS_PER_GROUP = EXPERTS_PER_GROUP * (EXPERTS_PER_GROUP - 1) // 2
N_BUCKETS = N_EXPERT_GROUPS * PAIRS_PER_GROUP
LN_EPS = 1e-5

WIDTH_A = N_DIL_GROUPS * HEADS_PER_DIL_GROUP * HEAD_DIM
GROUP_W = HEADS_PER_DIL_GROUP * HEAD_DIM
WIDTH_B = N_SB_HEADS * HEAD_DIM

BLK = 128
LANES = 128
SUBLANES = 8
DIL_RES_PER_STEP = 4
SB_Q = 256
SB_K = 256
SB_HEADS_PER_STEP = 4
LOG2_E = 1.4426950408889634
SB_DEAD_LOG2 = -151.0
ROUTER_LANE0 = N_EXPERT_GROUPS
EXPERT_TILE = 256
IN_PROJ_ROW_CHUNKS = 8
VMEM_LIMIT = 56 * 1024 * 1024
NEG_BIG = -1e30

f32 = jnp.float32
bf16 = jnp.bfloat16


def _rotate(t, cos_f, sin_f, lane):
    swapped = jnp.where(lane < ROPE_DIM // 2,
                        pltpu.roll(t, LANES - ROPE_DIM // 2, 1),
                        pltpu.roll(t, ROPE_DIM // 2, 1))
    return t * cos_f + swapped * sin_f


def _in_proj_kernel(x_ref, w_ref, *rest, dilation, n_rotary_tiles):
    if n_rotary_tiles:
        cos_ref, sin_ref, o_ref, *scratch = rest
    else:
        o_ref, *scratch = rest
    tm = x_ref.shape[0]
    w = w_ref[0].astype(bf16)
    if dilation == 1 and not n_rotary_tiles:
        o_ref[0, 0] = jnp.dot(x_ref[...], w, preferred_element_type=f32).astype(o_ref.dtype)
        return

    n_chunks = IN_PROJ_ROW_CHUNKS if tm % (IN_PROJ_ROW_CHUNKS * dilation * SUBLANES) == 0 else 1
    rows = tm // n_chunks
    n_slabs = w.shape[1] // LANES

    def matmul(c):
        return jnp.dot(x_ref[c * rows:(c + 1) * rows, :], w, preferred_element_type=f32)

    def epilogue(c, acc):
        slabs = [acc[:, s * LANES:(s + 1) * LANES] for s in range(n_slabs)]
        if n_rotary_tiles:
            cos_f = cos_ref[c * rows:(c + 1) * rows, :]
            sin_f = sin_ref[c * rows:(c + 1) * rows, :]
            lane = lax.broadcasted_iota(jnp.int32, cos_f.shape, 1)
            is_qk = pl.program_id(1) < n_rotary_tiles
            slabs = [jnp.where(is_qk, _rotate(slab, cos_f, sin_f, lane), slab) for slab in slabs]
        if dilation == 1:
            o_ref[0, 0, c * rows:(c + 1) * rows, :] = jnp.concatenate(slabs, axis=1).astype(o_ref.dtype)
            return
        (acc_ref,) = scratch
        n = rows // dilation
        for s, slab in enumerate(slabs):
            acc_ref[s, c * rows:(c + 1) * rows, :] = slab
        for r in range(dilation):
            for s in range(n_slabs):
                o_ref[0, r, c * n:(c + 1) * n, s * LANES:(s + 1) * LANES] = (
                    acc_ref[s, pl.ds(c * rows + r, n, stride=dilation), :].astype(o_ref.dtype))

    acc = matmul(0)
    for c in range(n_chunks):
        nxt = matmul(c + 1) if c + 1 < n_chunks else None
        epilogue(c, acc)
        acc = nxt


def _in_proj(x, w_in, layer, col_block, n_out, b, s, dilation, tm, tn, rope=None, n_rotary_tiles=0):
    k = x.shape[1]
    per_b = s // tm
    scratch = [] if dilation == 1 else [pltpu.VMEM((tn // LANES, tm, LANES), f32)]
    tables = list(rope) if n_rotary_tiles else []
    return pl.pallas_call(
        functools.partial(_in_proj_kernel, dilation=dilation, n_rotary_tiles=n_rotary_tiles),
        grid=(b * per_b, n_out // tn),
        in_specs=[pl.BlockSpec((tm, k), lambda i, j: (i, 0)),
                  pl.BlockSpec((1, k, tn), lambda i, j: (layer, 0, col_block(j)))]
                 + [pl.BlockSpec((tm, LANES), lambda i, j: (i, 0))] * len(tables),
        out_specs=pl.BlockSpec((1, dilation, tm // dilation, tn),
                               lambda i, j: (i // per_b, 0, i % per_b, j)),
        out_shape=jax.ShapeDtypeStruct((b, dilation, s // dilation, n_out), bf16),
        scratch_shapes=scratch,
        compiler_params=pltpu.CompilerParams(
            dimension_semantics=("parallel", "parallel"), vmem_limit_bytes=VMEM_LIMIT),
        name=f"in_proj_d{dilation}",
    )(x, w_in, *tables)


def _dil_kernel(q_ref, kc_ref, kp_ref, vc_ref, vp_ref, o_ref, lse_ref, o_scr, lse_scr,
                *, dilation, res_per_step):
    n = pl.program_id(1)
    rr = pl.program_id(2)
    qi = lax.broadcasted_iota(jnp.int32, (BLK, 2 * BLK), 0)
    kj = lax.broadcasted_iota(jnp.int32, (BLK, 2 * BLK), 1)
    first_key = jnp.where(n > 0, 0, BLK)
    valid = (kj >= jnp.maximum(qi, first_key)) & (kj <= qi + BLK)
    scale = HEAD_DIM ** -0.5

    def residue(ri, _):
        rows = pl.ds(rr * res_per_step + ri, BLK, stride=dilation) if dilation > 1 else slice(None)
        heads = range(HEADS_PER_DIL_GROUP)
        sls = [slice(h * HEAD_DIM, (h + 1) * HEAD_DIM) for h in heads]
        ss = [jnp.where(valid,
                        lax.dot_general(q_ref[0, ri, :, sl],
                                        jnp.concatenate([kp_ref[0, ri, :, sl], kc_ref[0, ri, :, sl]], axis=0),
                                        (((1,), (1,)), ((), ())), preferred_element_type=f32) * scale,
                        NEG_BIG)
              for sl in sls]
        ms = [jnp.max(s, axis=-1, keepdims=True) for s in ss]
        ps = [jnp.exp(s - m) for s, m in zip(ss, ms)]
        dens = [jnp.sum(p, axis=-1, keepdims=True) for p in ps]
        os_ = [jnp.dot(p.astype(bf16),
                       jnp.concatenate([vp_ref[0, ri, :, sl], vc_ref[0, ri, :, sl]], axis=0),
                       preferred_element_type=f32) for p, sl in zip(ps, sls)]
        for h in heads:
            o_scr[h, rows, :] = os_[h] / dens[h]
            lse_scr[h, rows, :] = jnp.broadcast_to(ms[h] + jnp.log(dens[h]), (BLK, HEAD_DIM))
        return 0

    lax.fori_loop(0, res_per_step, residue, 0)

    @pl.when(rr == pl.num_programs(2) - 1)
    def _():
        for h in range(HEADS_PER_DIL_GROUP):
            sl = slice(h * HEAD_DIM, (h + 1) * HEAD_DIM)
            o_ref[0, :, sl] = o_scr[h].astype(o_ref.dtype)
            lse_ref[0, :, sl] = lse_scr[h]


def _dilated_attention(qkv, dilation):
    b, _, sub, _ = qkv.shape
    s = sub * dilation
    nb = sub // BLK
    rs = min(dilation, DIL_RES_PER_STEP)

    def cur(c):
        return pl.BlockSpec((1, rs, BLK, GROUP_W), lambda bi, n, rr: (bi, rr, n, c))

    def prev(c):
        return pl.BlockSpec((1, rs, BLK, GROUP_W), lambda bi, n, rr: (bi, rr, jnp.maximum(n - 1, 0), c))

    out_spec = pl.BlockSpec((1, dilation * BLK, GROUP_W), lambda bi, n, rr: (bi, n, 0))
    o, lse = pl.pallas_call(
        functools.partial(_dil_kernel, dilation=dilation, res_per_step=rs),
        grid=(b, nb, dilation // rs),
        in_specs=[cur(0), cur(1), prev(1), cur(2), prev(2)],
        out_specs=[out_spec, out_spec],
        out_shape=[jax.ShapeDtypeStruct((b, s, GROUP_W), bf16),
                   jax.ShapeDtypeStruct((b, s, GROUP_W), f32)],
        scratch_shapes=[pltpu.VMEM((HEADS_PER_DIL_GROUP, dilation * BLK, HEAD_DIM), f32),
                        pltpu.VMEM((HEADS_PER_DIL_GROUP, dilation * BLK, HEAD_DIM), f32)],
        compiler_params=pltpu.CompilerParams(
            dimension_semantics=("parallel", "arbitrary", "arbitrary"), vmem_limit_bytes=VMEM_LIMIT),
        name=f"dilated_attn_d{dilation}",
    )(qkv, qkv, qkv, qkv, qkv)
    return o.reshape(b * s, GROUP_W), lse.reshape(b * s, GROUP_W)


def _sb_kernel(q_ref, k_ref, v_ref, up_ref, o_ref, acc_ref, c_ref):
    i = pl.program_id(2)
    scale2 = HEAD_DIM ** -0.5 * LOG2_E
    row = lax.broadcasted_iota(jnp.int32, (SB_Q, SB_K), 0)
    coli = lax.broadcasted_iota(jnp.int32, (SB_Q, SB_K), 1)
    causal = coli < row
    acc_ref[...] = jnp.zeros_like(acc_ref)
    c_ref[...] = jnp.zeros_like(c_ref)

    def chunk(j, diag):
        start = pl.multiple_of(j * SB_K, SB_K)
        heads = range(SB_HEADS_PER_STEP)
        sls = [slice(hd * HEAD_DIM, (hd + 1) * HEAD_DIM) for hd in heads]
        ts = [lax.dot_general(q_ref[0, :, sl], k_ref[0, pl.ds(start, SB_K), sl],
                              (((1,), (1,)), ((), ())), preferred_element_type=f32) * scale2
              for sl in sls]
        lss, lks, splits = [], [], []
        for t in ts:
            ls = jnp.minimum(t, 0.0) - jnp.log2(1.0 + jnp.exp2(-jnp.abs(t)))
            lk = ls - t
            if diag:
                lk = jnp.where(causal, lk, 0.0)
            lss.append(ls)
            lks.append(lk)
            splits.append(lk.astype(bf16))
        laters = [jnp.dot(x, up_ref[...], preferred_element_type=f32) for x in splits]
        probs = []
        for hd in heads:
            c = c_ref[hd]
            a = jnp.exp2(lss[hd] + laters[hd] + jnp.concatenate([c, c], axis=1))
            if diag:
                a = jnp.where(causal, a, 0.0)
            probs.append(a.astype(bf16))
            c_ref[hd] = c + jnp.sum(lks[hd], axis=1, keepdims=True)
        for hd in heads:
            acc_ref[hd] += jnp.dot(probs[hd], v_ref[0, pl.ds(start, SB_K), sls[hd]],
                                   preferred_element_type=f32)

    chunk(i, True)

    def live():
        return jnp.max(c_ref[...]) > SB_DEAD_LOG2

    def cond(state):
        t, go = state
        return jnp.logical_and(t < i, go)

    def body(state):
        t, _ = state
        chunk(i - 1 - t, False)
        return t + 1, live()

    lax.while_loop(cond, body, (jnp.int32(0), live()))
    for hd in range(SB_HEADS_PER_STEP):
        o_ref[0, :, hd * HEAD_DIM:(hd + 1) * HEAD_DIM] = acc_ref[hd].astype(o_ref.dtype)


def _stick_breaking(main3, col):
    b, s, _ = main3.shape
    w = SB_HEADS_PER_STEP * HEAD_DIM
    qb, kb, vb = col["qb"] // w, col["kb"] // w, col["vb"] // w
    jr = lax.broadcasted_iota(jnp.int32, (SB_K, SB_K), 0)
    sc = lax.broadcasted_iota(jnp.int32, (SB_K, SB_K), 1)
    upper2 = jnp.where(jr > sc, 1.0, 0.0).astype(bf16)
    o = pl.pallas_call(
        _sb_kernel,
        grid=(b, N_SB_HEADS // SB_HEADS_PER_STEP, s // SB_Q),
        in_specs=[pl.BlockSpec((1, SB_Q, w), lambda bi, h, i: (bi, i, qb + h)),
                  pl.BlockSpec((1, s, w), lambda bi, h, i: (bi, 0, kb + h)),
                  pl.BlockSpec((1, s, w), lambda bi, h, i: (bi, 0, vb + h)),
                  pl.BlockSpec((SB_K, SB_K), lambda bi, h, i: (0, 0))],
        out_specs=pl.BlockSpec((1, SB_Q, w), lambda bi, h, i: (bi, i, h)),
        out_shape=jax.ShapeDtypeStruct((b, s, WIDTH_B), bf16),
        scratch_shapes=[pltpu.VMEM((SB_HEADS_PER_STEP, SB_Q, HEAD_DIM), f32),
                        pltpu.VMEM((SB_HEADS_PER_STEP, SB_Q, HEAD_DIM), f32)],
        compiler_params=pltpu.CompilerParams(
            dimension_semantics=("parallel", "parallel", "arbitrary"), vmem_limit_bytes=VMEM_LIMIT),
        name="stick_breaking_attn",
    )(main3, main3, main3, upper2)
    return o.reshape(b * s, WIDTH_B)


def _layer_norm(u, g, b):
    mu = jnp.mean(u, axis=-1, keepdims=True)
    d = u - mu
    var = jnp.mean(d * d, axis=-1, keepdims=True)
    return d * lax.rsqrt(var + LN_EPS) * g + b


def _mix_kernel(o0_ref, o1_ref, o2_ref, l0_ref, l1_ref, l2_ref, ob_ref, ga_ref, gb_ref, h_ref,
                pa_ref, pb_ref, wo_ref, lng_ref, lnb_ref, wr_ref, br_ref,
                h1_ref, route_ref, cnt_ref, carry_ref, *, alpha, h_chunked):
    step = pl.program_id(0)

    @pl.when(step == 0)
    def _():
        carry_ref[...] = jnp.zeros_like(carry_ref)

    l0, l1, l2 = l0_ref[...], l1_ref[...], l2_ref[...]
    m = jnp.maximum(jnp.maximum(l0, l1), l2)
    e0, e1, e2 = jnp.exp(l0 - m), jnp.exp(l1 - m), jnp.exp(l2 - m)
    o_a = (e0 * o0_ref[...].astype(f32) + e1 * o1_ref[...].astype(f32)
           + e2 * o2_ref[...].astype(f32)) / (e0 + e1 + e2)
    ya = jnp.dot(o_a.astype(bf16), pa_ref[...], preferred_element_type=f32)
    yb = jnp.dot(ob_ref[...], pb_ref[...], preferred_element_type=f32)
    merged = (jax.nn.sigmoid(ga_ref[...].astype(f32)) * ya
              + jax.nn.sigmoid(gb_ref[...].astype(f32)) * yb)
    y = jnp.dot(merged.astype(bf16), wo_ref[...], preferred_element_type=f32)
    tm, d_model = y.shape
    n_chunks = d_model // LANES
    if h_chunked:
        h_res = jnp.concatenate(
            [h_ref[pl.ds(c, tm, stride=n_chunks), :] for c in range(n_chunks)], axis=1)
    else:
        h_res = h_ref[...]
    h1 = _layer_norm(alpha * h_res + y, lng_ref[...], lnb_ref[...])
    pitch = h1_ref.shape[0] // tm
    for c in range(n_chunks):
        h1_ref[pl.ds(c, tm, stride=pitch), :] = h1[:, c * LANES:(c + 1) * LANES]
    for c in range(n_chunks + 1, pitch):
        h1_ref[pl.ds(c, tm, stride=pitch), :] = jnp.zeros((tm, LANES), f32)
    h1b = h1.astype(bf16)

    logits = jnp.dot(h1b, wr_ref[...], preferred_element_type=f32) + br_ref[...]
    lane = lax.broadcasted_iota(jnp.int32, (tm, LANES), 1).astype(f32)
    no_lane = float(LANES)
    lc = jnp.where(lane < N_EXPERT_GROUPS, logits, NEG_BIG)
    mc = jnp.max(lc, axis=-1, keepdims=True)
    p_grp = 1.0 / jnp.sum(jnp.exp(lc - mc), axis=-1, keepdims=True)
    g_idx = jnp.min(jnp.where(lc == mc, lane, no_lane), axis=-1, keepdims=True)
    lo_lane = ROUTER_LANE0 + EXPERTS_PER_GROUP * g_idx
    sel = (lane >= lo_lane) & (lane < lo_lane + EXPERTS_PER_GROUP)
    lf = jnp.where(sel, logits, NEG_BIG)
    v1 = jnp.max(lf, axis=-1, keepdims=True)
    i1 = jnp.min(jnp.where(lf == v1, lane, no_lane), axis=-1, keepdims=True)
    lf2 = jnp.where(lane == i1, NEG_BIG, lf)
    v2 = jnp.max(lf2, axis=-1, keepdims=True)
    i2 = jnp.min(jnp.where((lf2 == v2) & sel & (lane != i1), lane, no_lane), axis=-1, keepdims=True)
    t = jnp.exp(v2 - v1)
    w1 = 1.0 / (1.0 + t)
    w2 = t / (1.0 + t)
    e1 = i1 - lo_lane
    e2 = i2 - lo_lane
    a = jnp.minimum(e1, e2)
    b = jnp.maximum(e1, e2)
    pair = a * (2.0 * EXPERTS_PER_GROUP - 1.0 - a) * 0.5 + (b - a - 1.0)
    bucket = PAIRS_PER_GROUP * g_idx + pair
    first_is_a = e1 < e2
    g1 = p_grp * w1
    g2 = p_grp * w2
    w_a = jnp.where(first_is_a, g1, g2)
    w_b = jnp.where(first_is_a, g2, g1)
    h1_ref[pl.ds(n_chunks, tm, stride=pitch), :] = jnp.where(
        lane == 0, w_a, jnp.where(lane == 1, w_b, 0.0))

    hit = lane == bucket
    onehot = jnp.where(hit, 1.0, 0.0)
    r = lax.broadcasted_iota(jnp.int32, (tm, tm), 0)
    c = lax.broadcasted_iota(jnp.int32, (tm, tm), 1)
    lower = jnp.where(c < r, 1.0, 0.0).astype(bf16)
    before = jnp.dot(lower, onehot.astype(bf16), preferred_element_type=f32) + carry_ref[0:1, :]
    rank = jnp.sum(jnp.where(hit, before, 0.0), axis=-1, keepdims=True)
    route = jnp.where(lane == 0, bucket, jnp.where(lane == 1, rank, 0.0))
    route_ref[...] = route.astype(jnp.int32)
    new_carry = carry_ref[0:1, :] + jnp.sum(onehot, axis=0, keepdims=True)
    carry_ref[...] = jnp.broadcast_to(new_carry, carry_ref.shape)
    cnt_ref[...] = jnp.broadcast_to(new_carry, cnt_ref.shape)


def _chunk_pitch(d):
    return -(-(d // LANES + 1) // SUBLANES) * SUBLANES


def _mix(outs, lses, o_b, main2, h, h_chunked, pa, pb, wo, lng, lnb, wr, br, col, alpha, tm):
    t = main2.shape[0]
    d = pa.shape[1]
    pitch = _chunk_pitch(d)
    row = lambda width, c: pl.BlockSpec((tm, width), lambda i: (i, c))
    const = lambda shape: pl.BlockSpec(shape, lambda i: (0, 0), pipeline_mode=pl.Buffered(1))
    h_spec = pl.BlockSpec((tm * (d // LANES), LANES), lambda i: (i, 0)) if h_chunked else row(d, 0)
    in_specs = ([row(GROUP_W, 0)] * 6
                + [row(WIDTH_B, 0), row(d, col["ga"] // d), row(d, col["gb"] // d), h_spec,
                   const(pa.shape), const(pb.shape), const(wo.shape),
                   const(lng.shape), const(lnb.shape), const(wr.shape), const(br.shape)])
    return pl.pallas_call(
        functools.partial(_mix_kernel, alpha=alpha, h_chunked=h_chunked),
        grid=(t // tm,),
        in_specs=in_specs,
        out_specs=[pl.BlockSpec((tm * pitch, LANES), lambda i: (i, 0)), row(LANES, 0),
                   pl.BlockSpec((8, LANES), lambda i: (0, 0))],
        out_shape=[jax.ShapeDtypeStruct((t * pitch, LANES), f32),
                   jax.ShapeDtypeStruct((t, LANES), jnp.int32),
                   jax.ShapeDtypeStruct((8, LANES), f32)],
        scratch_shapes=[pltpu.VMEM((8, LANES), f32)],
        compiler_params=pltpu.CompilerParams(
            dimension_semantics=("arbitrary",), vmem_limit_bytes=VMEM_LIMIT),
        name="mix_ln_router",
    )(*outs, *lses, o_b, main2, main2, h, pa, pb, wo, lng, lnb, wr, br)


def _gather_tokens(idx_ref, n_tok, pitch, src_hbm, dst, sem):
    def body(p, _):
        for k in range(2):
            tok = 2 * p + k
            start = pl.multiple_of(idx_ref[0, 0, tok], SUBLANES)
            pltpu.make_async_copy(src_hbm.at[pl.ds(start, pitch)],
                                  dst.at[pl.ds(pl.multiple_of(tok * pitch, SUBLANES), pitch)],
                                  sem).start(priority=k)
        return 0
    lax.fori_loop(0, n_tok // 2, body, 0, unroll=4)


def _scatter_tokens(idx_ref, n_tok, pitch, src, dst_hbm, sem):
    def body(p, _):
        for k in range(2):
            tok = 2 * p + k
            start = pl.multiple_of(idx_ref[0, 0, tok], SUBLANES)
            pltpu.make_async_copy(src.at[pl.ds(pl.multiple_of(tok * pitch, SUBLANES), pitch)],
                                  dst_hbm.at[pl.ds(start, pitch)], sem).start(priority=k)
        return 0
    lax.fori_loop(0, n_tok // 2, body, 0, unroll=4)


def _expert_kernel(ta_ref, tb_ref, nu_ref, src_cur, src_nxt, dst_cur, x_hbm,
                   wua_ref, wub_ref, wda_ref, wdb_ref, lng_ref, lnb_ref, o_hbm,
                   xbuf, obuf, gsem, ssem, *, alpha, pitch):
    i = pl.program_id(0)
    n = pl.num_programs(0)
    slot = i % 2
    ff = wda_ref.shape[1]
    d = wda_ref.shape[2]
    n_chunks = d // LANES
    x_rows = EXPERT_TILE * pitch
    o_rows = EXPERT_TILE * n_chunks

    n_used = nu_ref[0]

    @pl.when(i == 0)
    def _():
        _gather_tokens(src_cur, EXPERT_TILE, pitch, x_hbm, xbuf.at[0], gsem.at[0])

    @pl.when(i + 1 < n_used)
    def _():
        _gather_tokens(src_nxt, EXPERT_TILE, pitch, x_hbm, xbuf.at[1 - slot], gsem.at[1 - slot])

    @pl.when(i >= 2)
    def _():
        pltpu.make_async_copy(obuf.at[slot], o_hbm.at[pl.ds(0, o_rows)], ssem.at[slot]).wait()

    @pl.when(i < n_used)
    def _():
        pltpu.make_async_copy(x_hbm.at[pl.ds(0, x_rows)], xbuf.at[slot], gsem.at[slot]).wait()
        chunk = lambda c: xbuf[slot, pl.ds(c, EXPERT_TILE, stride=pitch), :]
        x = jnp.concatenate([chunk(c) for c in range(n_chunks)], axis=1)
        gates = chunk(n_chunks)
        xb = x.astype(bf16)

        ups = [jnp.dot(xb, wu_ref[0], preferred_element_type=f32) for wu_ref in (wua_ref, wub_ref)]
        acts = [(hh[:, :ff] * jax.nn.sigmoid(hh[:, :ff]) * hh[:, ff:]).astype(bf16) for hh in ups]
        y_a, y_b = [jnp.dot(act, wd_ref[0], preferred_element_type=f32)
                    for act, wd_ref in zip(acts, (wda_ref, wdb_ref))]
        y = gates[:, 0:1] * y_a + gates[:, 1:2] * y_b
        h2 = _layer_norm(alpha * x + y, lng_ref[...], lnb_ref[...])
        for c in range(n_chunks):
            obuf[slot, pl.ds(c, EXPERT_TILE, stride=n_chunks), :] = h2[:, c * LANES:(c + 1) * LANES]
        _scatter_tokens(dst_cur, EXPERT_TILE, n_chunks, obuf.at[slot], o_hbm, ssem.at[slot])

    @pl.when(i >= n_used)
    def _():
        pltpu.make_async_copy(obuf.at[slot], o_hbm.at[pl.ds(pl.multiple_of(i * o_rows, SUBLANES), o_rows)],
                              ssem.at[slot]).start()

    @pl.when(i == n - 1)
    def _():
        pltpu.make_async_copy(obuf.at[slot], o_hbm.at[pl.ds(0, o_rows)], ssem.at[slot]).wait()
        pltpu.make_async_copy(obuf.at[1 - slot], o_hbm.at[pl.ds(0, o_rows)], ssem.at[1 - slot]).wait()


def _experts(h1x, src, dst, tile_a, tile_b, n_used, wu, wd, layer, lng, lnb, alpha, n_tok_out):
    d = wd.shape[2]
    pitch = _chunk_pitch(d)
    n_chunks = d // LANES
    n_tiles = src.shape[0]
    ff2 = wu.shape[2]
    ff = wd.shape[1]
    e0 = layer * N_EXPERTS
    idx_spec = lambda f: pl.BlockSpec((1, 1, EXPERT_TILE), f, memory_space=pltpu.SMEM)
    const = lambda shape: pl.BlockSpec(shape, lambda i, ta, tb, nu: (0, 0))
    grid_spec = pltpu.PrefetchScalarGridSpec(
        num_scalar_prefetch=3,
        grid=(n_tiles,),
        in_specs=[
            idx_spec(lambda i, ta, tb, nu: (i, 0, 0)),
            idx_spec(lambda i, ta, tb, nu: (jnp.minimum(i + 1, n_tiles - 1), 0, 0)),
            idx_spec(lambda i, ta, tb, nu: (i, 0, 0)),
            pl.BlockSpec(memory_space=pl.ANY),
            pl.BlockSpec((1, d, ff2), lambda i, ta, tb, nu: (e0 + ta[i], 0, 0)),
            pl.BlockSpec((1, d, ff2), lambda i, ta, tb, nu: (e0 + tb[i], 0, 0)),
            pl.BlockSpec((1, ff, d), lambda i, ta, tb, nu: (e0 + ta[i], 0, 0)),
            pl.BlockSpec((1, ff, d), lambda i, ta, tb, nu: (e0 + tb[i], 0, 0)),
            const(lng.shape), const(lnb.shape),
        ],
        out_specs=pl.BlockSpec(memory_space=pl.ANY),
        scratch_shapes=[pltpu.VMEM((2, EXPERT_TILE * pitch, LANES), f32),
                        pltpu.VMEM((2, EXPERT_TILE * n_chunks, LANES), f32),
                        pltpu.SemaphoreType.DMA((2,)), pltpu.SemaphoreType.DMA((2,))],
    )
    return pl.pallas_call(
        functools.partial(_expert_kernel, alpha=alpha, pitch=pitch),
        grid_spec=grid_spec,
        out_shape=jax.ShapeDtypeStruct((n_tok_out * n_chunks, LANES), f32),
        compiler_params=pltpu.CompilerParams(
            dimension_semantics=("arbitrary",), vmem_limit_bytes=VMEM_LIMIT),
        name="experts_ln",
    )(tile_a, tile_b, n_used, src, src, dst, h1x, wu, wu, wd, wd, lng, lnb)


def _unchunk_kernel(h_ref, o_ref):
    tm, d = o_ref.shape
    n_chunks = d // LANES
    for c in range(n_chunks):
        o_ref[:, c * LANES:(c + 1) * LANES] = (
            h_ref[pl.ds(c, tm, stride=n_chunks), :].astype(o_ref.dtype))


def _unchunk(h, t, d, dtype, tm):
    n_chunks = d // LANES
    return pl.pallas_call(
        _unchunk_kernel,
        grid=(t // tm,),
        in_specs=[pl.BlockSpec((tm * n_chunks, LANES), lambda i: (i, 0))],
        out_specs=pl.BlockSpec((tm, d), lambda i: (i, 0)),
        out_shape=jax.ShapeDtypeStruct((t, d), dtype),
        compiler_params=pltpu.CompilerParams(
            dimension_semantics=("parallel",), vmem_limit_bytes=VMEM_LIMIT),
        name="unchunk",
    )(h)


def _rope_tables(positions):
    b, s = positions.shape
    half = ROPE_DIM // 2
    inv_freq = ROPE_THETA ** (-2.0 * jnp.arange(half, dtype=f32) / ROPE_DIM)
    ang = positions.astype(f32).reshape(b * s, 1) * inv_freq
    cos, sin = jnp.cos(ang), jnp.sin(ang)
    pad = LANES - ROPE_DIM
    cos_f = jnp.concatenate([cos, cos, jnp.ones((b * s, pad), f32)], axis=-1)
    sin_f = jnp.concatenate([-sin, sin, jnp.zeros((b * s, pad), f32)], axis=-1)
    return cos_f, sin_f


def _bucket_experts():
    a_ids, b_ids = [], []
    for g in range(N_EXPERT_GROUPS):
        for a in range(EXPERTS_PER_GROUP):
            for b in range(a + 1, EXPERTS_PER_GROUP):
                a_ids.append(g * EXPERTS_PER_GROUP + a)
                b_ids.append(g * EXPERTS_PER_GROUP + b)
    return jnp.array(a_ids, jnp.int32), jnp.array(b_ids, jnp.int32)


def _route_plan(route, cnt, t, n_tiles):
    bucket = route[:, 0]
    rank = route[:, 1]
    counts = cnt[0, :N_BUCKETS].astype(jnp.int32)
    padded = (counts + EXPERT_TILE - 1) // EXPERT_TILE * EXPERT_TILE
    ends = jnp.cumsum(padded)
    offs = ends - padded
    pos = offs[bucket] + rank
    n_used = (ends[-1] // EXPERT_TILE).astype(jnp.int32)
    tile_start = jnp.arange(n_tiles, dtype=jnp.int32) * EXPERT_TILE
    tile_bucket = jnp.sum((tile_start[:, None] >= ends[None, :]).astype(jnp.int32), axis=1)
    tile_bucket = jnp.minimum(tile_bucket, N_BUCKETS - 1)
    last_used = tile_bucket[jnp.maximum(n_used - 1, 0)]
    tile_bucket = jnp.where(jnp.arange(n_tiles) < n_used, tile_bucket, last_used)
    a_ids, b_ids = _bucket_experts()
    n_slots = n_tiles * EXPERT_TILE
    slot = jnp.arange(n_slots, dtype=jnp.int32)
    src = (slot % t).at[pos].set(jnp.arange(t, dtype=jnp.int32))
    slot_bucket = jnp.repeat(tile_bucket, EXPERT_TILE)
    valid = (slot < ends[-1]) & (slot - offs[slot_bucket] < counts[slot_bucket])
    dummy = t + slot - jnp.cumsum(counts)[slot_bucket]
    dst = jnp.where(valid, src, dummy).astype(jnp.int32)
    shape = (n_tiles, 1, EXPERT_TILE)
    return (src.reshape(shape), dst.reshape(shape), a_ids[tile_bucket], b_ids[tile_bucket],
            n_used.reshape(1))


def kernel(x, positions, w_in, p_a, p_b, w_o, ln_g, ln_b, router_coarse, router_coarse_bias,
           router_fine, router_fine_bias, w_up, w_down):
    b, s, d = x.shape
    depth = w_in.shape[0]
    t = b * s
    alpha = float((2 * depth) ** 0.25)
    assert s % (DIL_CONFIGS[-1][1] * BLK) == 0 and s % SB_Q == 0 and d % GROUP_W == 0
    assert t % EXPERT_TILE == 0 and t // EXPERT_TILE >= 2

    sizes = (("ga", d), ("gb", d), ("qb", WIDTH_B), ("kb", WIDTH_B), ("vb", WIDTH_B))
    col, off = {}, 0
    for name, width in sizes:
        col[name] = off
        off += width
    a3, b3 = 3 * WIDTH_A, 3 * WIDTH_B

    rope = _rope_tables(positions)
    tm_proj = min(2048, s)
    tm_mix = min(256, t)
    n_tiles = t // EXPERT_TILE + N_BUCKETS
    n_chunks = d // LANES
    wu_all = w_up.astype(bf16).reshape((depth * N_EXPERTS,) + w_up.shape[2:])
    wd_all = w_down.astype(bf16).reshape((depth * N_EXPERTS,) + w_down.shape[2:])

    h = x.reshape(t, d)
    hb = h.astype(bf16)
    for l in range(depth):
        gate_blocks = 2 * d // GROUP_W
        main_block = lambda j: jnp.where(j < gate_blocks, (a3 + b3) // GROUP_W + j,
                                         a3 // GROUP_W + j - gate_blocks)
        main = _in_proj(hb, w_in, l, main_block, 2 * d + b3, b, s, 1, tm_proj, GROUP_W)
        main2 = main.reshape(t, main.shape[-1])
        outs, lses = [], []
        for g, (_, dilation) in enumerate(DIL_CONFIGS):
            group_block = lambda j, g=g: j * N_DIL_GROUPS + g
            qkv = _in_proj(hb, w_in, l, group_block, 3 * GROUP_W, b, s, dilation, tm_proj, GROUP_W,
                           rope=rope, n_rotary_tiles=2)
            o, lse = _dilated_attention(qkv, dilation)
            outs.append(o)
            lses.append(lse)
        o_b = _stick_breaking(main.reshape(b, s, main.shape[-1]), col)

        wr = jnp.zeros((d, LANES), f32)
        wr = wr.at[:, :N_EXPERT_GROUPS].set(router_coarse[l])
        wr = wr.at[:, ROUTER_LANE0:ROUTER_LANE0 + N_EXPERTS].set(router_fine[l]).astype(bf16)
        br = jnp.zeros((1, LANES), f32)
        br = br.at[0, :N_EXPERT_GROUPS].set(router_coarse_bias[l])
        br = br.at[0, ROUTER_LANE0:ROUTER_LANE0 + N_EXPERTS].set(router_fine_bias[l])
        h1x, route, cnt = _mix(
            outs, lses, o_b, main2, h, l > 0, p_a[l].astype(bf16), p_b[l].astype(bf16),
            w_o[l].astype(bf16), ln_g[l, 0].reshape(1, d), ln_b[l, 0].reshape(1, d), wr, br, col,
            alpha, tm_mix)

        src, dst, tile_a, tile_b, n_used = _route_plan(route, cnt, t, n_tiles)
        h = _experts(h1x, src * _chunk_pitch(d), dst * n_chunks, tile_a, tile_b, n_used, wu_all, wd_all,
                     l, ln_g[l, 1].reshape(1, d), ln_b[l, 1].reshape(1, d), alpha,
                     n_tiles * EXPERT_TILE)
        if l + 1 < depth:
            hb = _unchunk(h, t, d, bf16, min(512, t))
    return _unchunk(h, t, d, f32, min(512, t)).reshape(b, s, d)
```
